```python
import math
import jax, jax.numpy as jnp
from jax import lax
import numpy as np

D_MODEL = 1024
BATCH = 8
SEQ = 8192
DEPTH = 2
DEC_BATCH = 16
DEC_SEQ = 16
PAST_LEN = 1024

CHUNK = 64
EPS = 1e-6
N_AB_LAYERS = (DEPTH + 1) // 2
N_SB_LAYERS = DEPTH // 2

GLA_HEADS = 4
GLA_DK = 64
GLA_DV = 128
GLA_RANK = 16
GLA_TAU = 16.0
GLA_WIDTH = GLA_HEADS * GLA_DV

DN_HEADS = 4
DN_DK = 128
DN_DV = 128
DN_WIDTH = DN_HEADS * DN_DV
CONV_W = 4
DN_CONV_CH = 2 * DN_HEADS * DN_DK + DN_WIDTH

AB_SPLITS = (GLA_HEADS * GLA_DK, GLA_HEADS * GLA_DK, GLA_WIDTH, GLA_RANK, GLA_WIDTH,
             DN_CONV_CH, DN_HEADS, DN_HEADS, DN_WIDTH)
AB_IN = sum(AB_SPLITS)
AB_MIX = GLA_WIDTH + DN_WIDTH

SB_HEADS = 16
SB_DH = 64
SB_WIDTH = SB_HEADS * SB_DH
SB_BLOCK = 128

kernel_name = "hybrid_gla_gdn_stickbreak_stream_step"


def rmsnorm(x, g):
    xf = x.astype(jnp.float32)
    y = xf * lax.rsqrt(jnp.mean(xf * xf, axis=-1, keepdims=True) + EPS)
    return (y * g.astype(jnp.float32)).astype(x.dtype)


def split_cols(p, sizes):
    idx = np.cumsum(sizes)[:-1].tolist()
    return jnp.split(p, idx, axis=-1)


def to_heads(t, n_heads):
    b, l, _ = t.shape
    return t.reshape(b, l, n_heads, -1).transpose(0, 2, 1, 3)


def from_heads(t):
    b, h, l, d = t.shape
    return t.transpose(0, 2, 1, 3).reshape(b, l, h * d)


def l2norm(t):
    return t * lax.rsqrt(jnp.sum(t * t, axis=-1, keepdims=True) + EPS)


def causal_conv_silu(u, w, hist):
    full = jnp.concatenate([hist.astype(u.dtype), u], axis=1)
    L = u.shape[1]
    out = sum(full[:, i:i + L] * w[i].astype(u.dtype) for i in range(CONV_W))
    return jax.nn.silu(out), full[:, full.shape[1] - (CONV_W - 1):]


def blocked_scan(step, s0, seqs, block):
    n = seqs[0].shape[2] // block

    def split(a):
        a = a.reshape(a.shape[:2] + (n, block) + a.shape[3:])
        return jnp.moveaxis(a, 2, 0)

    s_last, o = lax.scan(lambda s, xs: step(s, *xs), s0, tuple(split(a) for a in seqs))
    o = jnp.moveaxis(o, 0, 2)
    return s_last, o.reshape(o.shape[:2] + (n * block,) + o.shape[4:])


def gla_block(S, q, k, v, lg):
    C = q.shape[2]
    b = jnp.cumsum(lg, axis=2)
    incl = jnp.tril(jnp.ones((C, C), dtype=bool))
    diff = b[:, :, :, None, :] - b[:, :, None, :, :]
    dec = jnp.exp(jnp.where(incl[:, :, None], diff, -jnp.inf))
    attn = jnp.einsum('bhtd,bhsd,bhtsd->bhts', q, k, dec)
    o = (jnp.einsum('bhts,bhsv->bhtv', attn, v)
         + jnp.einsum('bhtd,bhdv->bhtv', q * jnp.exp(b), S))
    b_last = b[:, :, -1]
    S_new = (jnp.exp(b_last)[..., None] * S
             + jnp.einsum('bhsd,bhsv->bhdv', k * jnp.exp(b_last[:, :, None] - b), v))
    return S_new, o


def delta_block(S, q, k, v, beta, lg):
    C = q.shape[2]
    b = jnp.cumsum(lg, axis=-1)
    incl = jnp.tril(jnp.ones((C, C), dtype=bool))
    strict = jnp.tril(jnp.ones((C, C), dtype=bool), -1)
    dec = jnp.exp(jnp.where(incl, b[..., :, None] - b[..., None, :], -jnp.inf))
    kk = jnp.einsum('bhtd,bhsd->bhts', k, k)
    M = jnp.where(strict, beta[..., :, None] * dec * kk, 0.0)
    lhs = jnp.eye(C, dtype=M.dtype) + M
    rhs = beta[..., None] * (v - jnp.exp(b)[..., None] * jnp.einsum('bhtd,bhdv->bhtv', k, S))
    u = lax.linalg.triangular_solve(lhs, rhs, left_side=True, lower=True)
    qk = jnp.einsum('bhtd,bhsd->bhts', q, k) * dec
    o = (jnp.exp(b)[..., None] * jnp.einsum('bhtd,bhdv->bhtv', q, S)
         + jnp.einsum('bhts,bhsv->bhtv', qk, u))
    b_last = b[..., -1]
    S_new = (jnp.exp(b_last)[..., None, None] * S
             + jnp.einsum('bhsd,bhsv->bhdv', k * jnp.exp(b_last[..., None] - b)[..., None], u))
    return S_new, o


def ab_mixer(h, s_gla, s_dn, conv_hist, w_in, w_gate, b_gate, g_gla, conv_w, a_log, dt_bias, g_dn, w_out, block):
    f32 = jnp.float32
    p = h @ w_in
    gq, gk, gv, gr, gz, dqkv, dbeta, da, dz = split_cols(p, AB_SPLITS)
    q = to_heads(gq, GLA_HEADS).astype(f32) * (GLA_DK ** -0.5)
    k = to_heads(gk, GLA_HEADS).astype(f32)
    v = to_heads(gv, GLA_HEADS).astype(f32)
    lg = to_heads(jax.nn.log_sigmoid((gr @ w_gate + b_gate).astype(f32)) / GLA_TAU, GLA_HEADS)
    s_gla_new, o_gla = blocked_scan(gla_block, s_gla.astype(f32), (q, k, v, lg), block)
    c, conv_new = causal_conv_silu(dqkv, conv_w, conv_hist)
    cq, ck, cv = split_cols(c, (DN_HEADS * DN_DK, DN_HEADS * DN_DK, DN_WIDTH))
    q2 = l2norm(to_heads(cq, DN_HEADS).astype(f32)) * (DN_DK ** -0.5)
    k2 = l2norm(to_heads(ck, DN_HEADS).astype(f32))
    v2 = to_heads(cv, DN_HEADS).astype(f32)
    beta = jax.nn.sigmoid(dbeta.astype(f32)).transpose(0, 2, 1)
    lg2 = (-jnp.exp(a_log.astype(f32))
           * jax.nn.softplus(da.astype(f32) + dt_bias.astype(f32))).transpose(0, 2, 1)
    s_dn_new, o_dn = blocked_scan(delta_block, s_dn.astype(f32), (q2, k2, v2, beta, lg2), block)
    mix = jnp.concatenate([
        from_heads(rmsnorm(o_gla, g_gla)).astype(h.dtype) * jax.nn.silu(gz),
        from_heads(rmsnorm(o_dn, g_dn)).astype(h.dtype) * jax.nn.silu(dz)], axis=-1)
    return mix @ w_out, s_gla_new, s_dn_new, conv_new


def stick_breaking(q, k, v, q_start):
    z = jnp.einsum('bhtd,bhsd->bhts', q, k).astype(jnp.float32) * (SB_DH ** -0.5)
    qpos = q_start + jnp.arange(q.shape[2])
    kpos = jnp.arange(k.shape[2])
    mask = kpos[None, :] < qpos[:, None]
    log_rest = jnp.where(mask, jax.nn.log_sigmoid(-z), 0.0)
    rest_after = lax.cumsum(log_rest, axis=3, reverse=True) - log_rest
    w = jnp.where(mask, jnp.exp(jax.nn.log_sigmoid(z) + rest_after), 0.0)
    return jnp.einsum('bhts,bhsd->bhtd', w.astype(v.dtype), v)


def sb_project(h, w_in):
    q, k, v, z = split_cols(h @ w_in, (SB_WIDTH, SB_WIDTH, SB_WIDTH, SB_WIDTH))
    return to_heads(q, SB_HEADS), to_heads(k, SB_HEADS), to_heads(v, SB_HEADS), z


def sb_output(o, z, w_out):
    return (from_heads(o).astype(z.dtype) * jax.nn.silu(z)) @ w_out


def setup_inputs(seed: int = 0) -> dict:
    key = jax.random.key(seed)
    ks = jax.random.split(key, 24)
    f32 = jnp.float32

    def nrm(k, shape, s):
        return jax.random.normal(k, shape, f32) * s

    dt = jnp.exp(jax.random.uniform(ks[13], (N_AB_LAYERS, DN_HEADS), f32,
                                    minval=math.log(1e-3), maxval=math.log(1e-1)))
    return {
        "x_prompt": nrm(ks[0], (BATCH, SEQ, D_MODEL), 1.0),
        "x_sample": nrm(ks[1], (DEC_BATCH, DEC_SEQ, D_MODEL), 1.0),
        "state_gla": nrm(ks[2], (N_AB_LAYERS, DEC_BATCH, GLA_HEADS, GLA_DK, GLA_DV), 1.0),
        "state_delta": nrm(ks[3], (N_AB_LAYERS, DEC_BATCH, DN_HEADS, DN_DK, DN_DV), 0.3),
        "state_conv": nrm(ks[4], (N_AB_LAYERS, DEC_BATCH, CONV_W - 1, DN_CONV_CH), 1.0),
        "cache_k": nrm(ks[5], (N_SB_LAYERS, DEC_BATCH, SB_HEADS, PAST_LEN, SB_DH), 1.0),
        "cache_v": nrm(ks[6], (N_SB_LAYERS, DEC_BATCH, SB_HEADS, PAST_LEN, SB_DH), 1.0),
        "ab_norm": 1.0 + nrm(ks[7], (N_AB_LAYERS, D_MODEL), 0.01),
        "ab_w_in": nrm(ks[8], (N_AB_LAYERS, D_MODEL, AB_IN), D_MODEL ** -0.5),
        "gla_w_gate": nrm(ks[9], (N_AB_LAYERS, GLA_RANK, GLA_HEADS * GLA_DK), GLA_RANK ** -0.5),
        "gla_b_gate": nrm(ks[10], (N_AB_LAYERS, GLA_HEADS * GLA_DK), 0.1),
        "gla_out_norm": 1.0 + nrm(ks[11], (N_AB_LAYERS, GLA_DV), 0.01),
        "dn_conv_w": nrm(ks[12], (N_AB_LAYERS, CONV_W, DN_CONV_CH), 0.5),
        "dn_a_log": jnp.log(jax.random.uniform(ks[14], (N_AB_LAYERS, DN_HEADS), f32, minval=1.0, maxval=16.0)),
        "dn_dt_bias": dt + jnp.log(-jnp.expm1(-dt)),
        "dn_out_norm": 1.0 + nrm(ks[15], (N_AB_LAYERS, DN_DV), 0.01),
        "ab_w_out": nrm(ks[16], (N_AB_LAYERS, AB_MIX, D_MODEL), AB_MIX ** -0.5),
        "sb_norm": 1.0 + nrm(ks[17], (N_SB_LAYERS, D_MODEL), 0.01),
        "sb_w_in": nrm(ks[18], (N_SB_LAYERS, D_MODEL, 4 * SB_WIDTH), D_MODEL ** -0.5),
        "sb_w_out": nrm(ks[19], (N_SB_LAYERS, SB_WIDTH, D_MODEL), SB_WIDTH ** -0.5),
        "final_norm": 1.0 + nrm(ks[20], (D_MODEL,), 0.01),
    }


def reference(x_prompt, x_sample, state_gla, state_delta, state_conv, cache_k, cache_v,
              ab_norm, ab_w_in, gla_w_gate, gla_b_gate, gla_out_norm, dn_conv_w, dn_a_log, dn_dt_bias,
              dn_out_norm, ab_w_out, sb_norm, sb_w_in, sb_w_out, final_norm):
    hp, hs = x_prompt, x_sample
    bp = x_prompt.shape[0]
    seq_p = x_prompt.shape[1]
    seq_s = x_sample.shape[1]
    gla_p, gla_s, dn_p, dn_s, cv_p, cv_s = [], [], [], [], [], []
    kp_l, ks_l, vp_l, vs_l = [], [], [], []
    for layer in range(DEPTH):
        i = layer // 2
        if layer % 2 == 0:
            w = (ab_w_in[i], gla_w_gate[i], gla_b_gate[i], gla_out_norm[i], dn_conv_w[i],
                 dn_a_log[i], dn_dt_bias[i], dn_out_norm[i], ab_w_out[i])
            yp, sg, sd, sc = ab_mixer(
                rmsnorm(hp, ab_norm[i]),
                jnp.zeros((bp, GLA_HEADS, GLA_DK, GLA_DV), jnp.float32),
                jnp.zeros((bp, DN_HEADS, DN_DK, DN_DV), jnp.float32),
                jnp.zeros((bp, CONV_W - 1, DN_CONV_CH), hp.dtype),
                *w, CHUNK)
            ys, sg2, sd2, sc2 = ab_mixer(
                rmsnorm(hs, ab_norm[i]), state_gla[i], state_delta[i], state_conv[i], *w, seq_s)
            hp = hp + yp.astype(hp.dtype)
            hs = hs + ys.astype(hs.dtype)
            gla_p.append(sg); gla_s.append(sg2)
            dn_p.append(sd); dn_s.append(sd2)
            cv_p.append(sc); cv_s.append(sc2)
        else:
            q, k, v, z = sb_project(rmsnorm(hp, sb_norm[i]), sb_w_in[i])
            o = jnp.concatenate(
                [stick_breaking(q[:, :, s:s + SB_BLOCK], k[:, :, :s + SB_BLOCK], v[:, :, :s + SB_BLOCK], s)
                 for s in range(0, seq_p, SB_BLOCK)], axis=2)
            hp = hp + sb_output(o, z, sb_w_out[i]).astype(hp.dtype)
            kp_l.append(k); vp_l.append(v)
            q2, k2, v2, z2 = sb_project(rmsnorm(hs, sb_norm[i]), sb_w_in[i])
            k_all = jnp.concatenate([cache_k[i].astype(k2.dtype), k2], axis=2)
            v_all = jnp.concatenate([cache_v[i].astype(v2.dtype), v2], axis=2)
            o2 = stick_breaking(q2, k_all, v_all, cache_k.shape[3])
            hs = hs + sb_output(o2, z2, sb_w_out[i]).astype(hs.dtype)
            ks_l.append(k2); vs_l.append(v2)
    y_prompt = rmsnorm(hp, final_norm)
    y_sample = rmsnorm(hs, final_norm)
    return (y_prompt, y_sample,
            jnp.stack(gla_p), jnp.stack(gla_s),
            jnp.stack(dn_p), jnp.stack(dn_s),
            jnp.stack(cv_p), jnp.stack(cv_s),
            jnp.stack(kp_l), jnp.stack(ks_l),
            jnp.stack(vp_l), jnp.stack(vs_l))
```

```python
import functools

import jax
import jax.numpy as jnp
from jax import lax
from jax.experimental import pallas as pl
from jax.experimental.pallas import tpu as pltpu

F32 = jnp.float32
BF16 = jnp.bfloat16

EPS = 1e-6
CHUNK = 64
GLA_HEADS = 4
GLA_DK = 64
GLA_DV = 128
GLA_RANK = 16
GLA_TAU = 16.0
DN_HEADS = 4
DN_DK = 128
DN_DV = 128
CONV_W = 4
SB_HEADS = 16
SB_DH = 64

LANES = 128
SUBLANES = 8
INV_BLOCK = 16
VMEM_LIMIT = 56 * 1024 * 1024

_GQ = GLA_HEADS * GLA_DK
_GV = GLA_HEADS * GLA_DV
_DC = 2 * DN_HEADS * DN_DK + DN_HEADS * DN_DV
_DW = DN_HEADS * DN_DV
COL_QK = 0
COL_GV = 2 * _GQ
COL_GZ = COL_GV + _GV
COL_DC = COL_GZ + _GV
COL_DZ = COL_DC + _DC
COL_SM = COL_DZ + _DW
AB_COLS = 3840
SM_RANK = 0
SM_BETA = GLA_RANK
SM_DEC = GLA_RANK + DN_HEADS


def _sigmoid(x):
    return 1.0 / (1.0 + jnp.exp(-x))


def _softplus(x):
    return jnp.maximum(x, 0.0) + jnp.log(1.0 + jnp.exp(-jnp.abs(x)))


def _split(a):
    hi = a.astype(BF16)
    lo = (a - hi.astype(F32)).astype(BF16)
    return hi, lo


def _dot(a, b):
    return jnp.dot(a, b, preferred_element_type=F32)


def _dot_nt(a, b):
    return lax.dot_general(a, b, (((1,), (1,)), ((), ())), preferred_element_type=F32)


def _dot_tn(a, b):
    return lax.dot_general(a, b, (((0,), (0,)), ((), ())), preferred_element_type=F32)


def _dot3(a, b):
    ah, al = a
    bh, bl = b
    return _dot(ah, bh) + _dot(ah, bl) + _dot(al, bh)


def _rms(x, g):
    return x * lax.rsqrt(jnp.mean(x * x, axis=-1, keepdims=True) + EPS) * g


def _linear_kernel(*refs, n_x, has_norm, has_gate, has_res, has_post):
    refs = list(refs)
    xs = [refs.pop(0) for _ in range(n_x)]
    g_ref = refs.pop(0) if has_norm else None
    z_ref = refs.pop(0) if has_gate else None
    w_ref = refs.pop(0)
    r_ref = refs.pop(0) if has_res else None
    p_ref = refs.pop(0) if has_post else None
    o_ref = refs.pop(0)
    x = xs[0][...] if n_x == 1 else jnp.concatenate([r[...] for r in xs], axis=-1)
    if has_norm:
        x = _rms(x, g_ref[...])
    if has_gate:
        z = z_ref[...]
        x = x * (z * _sigmoid(z))
    y = _dot(x.astype(BF16), w_ref[...])
    if has_res:
        y = y + r_ref[...]
    if has_post:
        y = _rms(y, p_ref[...])
    o_ref[...] = y


def _linear(xs, w, *, norm=None, gate=None, res=None, post=None, tm, name):
    n = xs[0][0].shape[0]
    k, m = w.shape
    tm = min(tm, n)
    assert n % tm == 0
    in_specs, args = [], []
    kx = k // len(xs)
    for arr, cb in xs:
        in_specs.append(pl.BlockSpec((tm, kx), lambda i, cb=cb: (i, cb)))
        args.append(arr)
    if norm is not None:
        in_specs.append(pl.BlockSpec((1, k), lambda i: (0, 0)))
        args.append(norm.reshape(1, k))
    if gate is not None:
        in_specs.append(pl.BlockSpec((tm, k), lambda i, cb=gate[1]: (i, cb)))
        args.append(gate[0])
    in_specs.append(pl.BlockSpec((k, m), lambda i: (0, 0)))
    args.append(w)
    if res is not None:
        in_specs.append(pl.BlockSpec((tm, m), lambda i, cb=res[1]: (i, cb)))
        args.append(res[0])
    if post is not None:
        in_specs.append(pl.BlockSpec((1, m), lambda i: (0, 0)))
        args.append(post.reshape(1, m))
    kern = functools.partial(_linear_kernel, n_x=len(xs), has_norm=norm is not None,
                             has_gate=gate is not None, has_res=res is not None,
                             has_post=post is not None)
    return pl.pallas_call(
        kern,
        grid=(n // tm,),
        in_specs=in_specs,
        out_specs=pl.BlockSpec((tm, m), lambda i: (i, 0)),
        out_shape=jax.ShapeDtypeStruct((n, m), F32),
        compiler_params=pltpu.CompilerParams(dimension_semantics=("parallel",),
                                             vmem_limit_bytes=VMEM_LIMIT),
        name=name,
    )(*args)


def _gla_kernel(qk_ref, v_ref, z_ref, sm_ref, wg_ref, bg_ref, g_ref, s0_ref,
                mix_ref, s_out_ref, s_scr, *, C):
    c = pl.program_id(1)

    @pl.when(c == 0)
    def _():
        s_scr[...] = s0_ref[0]

    row = lax.broadcasted_iota(jnp.int32, (C, C), 0)
    col = lax.broadcasted_iota(jnp.int32, (C, C), 1)
    incl = row >= col
    ltri = jnp.where(incl, 1.0, 0.0).astype(BF16)

    sm = sm_ref[0]
    x = _dot(sm.astype(BF16), wg_ref[...]) + bg_ref[...]
    lg = -_softplus(-x) * (1.0 / GLA_TAU)
    lgh, lgl = _split(lg)
    b = _dot(ltri, lgh) + _dot(ltri, lgl)
    mid = C // 2 - 1
    bmid = b[mid:mid + 1, :]
    blast = b[C - 1:C, :]
    qk = qk_ref[0]
    q = qk[:, :_GQ] * (GLA_DK ** -0.5)
    k = qk[:, _GQ:]
    qe = (q * jnp.exp(b - bmid)).astype(BF16)
    ke = (k * jnp.exp(bmid - b)).astype(BF16)
    qd = (q * jnp.exp(b)).astype(BF16)
    kd = (k * jnp.exp(blast - b)).astype(BF16)
    dlast = jnp.exp(blast)
    v = v_ref[0]
    z = z_ref[0]
    g = g_ref[...]
    for h in range(GLA_HEADS):
        ks = slice(h * GLA_DK, (h + 1) * GLA_DK)
        vs = slice(h * GLA_DV, (h + 1) * GLA_DV)
        a = jnp.where(incl, _dot_nt(qe[:, ks], ke[:, ks]), 0.0)
        st = s_scr[h]
        vh = v[:, vs].astype(BF16)
        o = _dot(a.astype(BF16), vh) + _dot_nt(qd[:, ks], st.astype(BF16))
        s_scr[h] = st * dlast[:, ks] + _dot_tn(vh, kd[:, ks])
        zh = z[:, vs]
        mix_ref[0, :, vs] = _rms(o, g) * (zh * _sigmoid(zh))

    @pl.when(c == pl.num_programs(1) - 1)
    def _():
        s_out_ref[0] = s_scr[...]


def _gla(p3, wg, bg, g, s0t, C):
    B, L, _ = p3.shape
    nc = L // C
    blk = lambda w, cb: pl.BlockSpec((1, C, w), lambda b, c, cb=cb: (b, c, cb))
    full = lambda shp: pl.BlockSpec(shp, lambda b, c: (0,) * len(shp))
    st_spec = pl.BlockSpec((1, GLA_HEADS, GLA_DV, GLA_DK), lambda b, c: (b, 0, 0, 0))
    return pl.pallas_call(
        functools.partial(_gla_kernel, C=C),
        grid=(B, nc),
        in_specs=[blk(2 * _GQ, COL_QK // (2 * _GQ)), blk(_GV, COL_GV // _GV), blk(_GV, COL_GZ // _GV),
                  blk(LANES, COL_SM // LANES), full((LANES, _GQ)), full((1, _GQ)), full((1, GLA_DV)),
                  st_spec],
        out_specs=[pl.BlockSpec((1, C, _GV), lambda b, c: (b, c, 0)), st_spec],
        out_shape=[jax.ShapeDtypeStruct((B, L, _GV), F32),
                   jax.ShapeDtypeStruct((B, GLA_HEADS, GLA_DV, GLA_DK), F32)],
        scratch_shapes=[pltpu.VMEM((GLA_HEADS, GLA_DV, GLA_DK), F32)],
        compiler_params=pltpu.CompilerParams(dimension_semantics=("parallel", "arbitrary"),
                                             vmem_limit_bytes=VMEM_LIMIT),
        name="gla",
    )(p3, p3, p3, p3, wg, bg, g, s0t)


def _unit_lower_inverse(p, row, col, C):
    eye = jnp.where(row == col, 1.0, 0.0)
    nb = C // INV_BLOCK
    if nb > 1:
        same = (row // INV_BLOCK) == (col // INV_BLOCK)
        pd = jnp.where(same, p, 0.0)
    else:
        pd = p
    x = eye + pd
    pw = _split(pd)
    steps = INV_BLOCK.bit_length() - 1
    for s in range(1, steps):
        pw_f = _dot3(pw, pw)
        pw = _split(pw_f)
        x = x + _dot3(pw, _split(x))
    if nb == 1:
        return x
    assert nb <= 4
    xs = _split(x)
    n1 = _dot3(xs, _split(p - pd))
    n1s = _split(n1)
    n2 = _dot3(n1s, n1s)
    y = x + _dot3(n1s, xs)
    return y + _dot3(_split(n2), _split(y))


def _gdn_kernel(u_ref, sm_ref, z_ref, cw_ref, hist_ref, alog_ref, dtb_ref, g_ref, s0_ref,
                mix_ref, s_out_ref, conv_out_ref, s_scr, ubuf, *, C):
    c = pl.program_id(1)
    T8 = SUBLANES

    @pl.when(c == 0)
    def _():
        s_scr[...] = s0_ref[0]
        ubuf[0:T8, :] = hist_ref[0]

    u = u_ref[0]
    ubuf[T8:T8 + C, :] = u
    cw = cw_ref[...]
    conv = u * cw[CONV_W - 1:CONV_W, :]
    for i in range(CONV_W - 1):
        off = T8 - (CONV_W - 1) + i
        conv = conv + ubuf[off:off + C, :] * cw[i:i + 1, :]
    cs = conv * _sigmoid(conv)
    tail = u[C - T8:, :]
    ubuf[0:T8, :] = tail

    @pl.when(c == pl.num_programs(1) - 1)
    def _():
        conv_out_ref[0] = tail

    row = lax.broadcasted_iota(jnp.int32, (C, C), 0)
    col = lax.broadcasted_iota(jnp.int32, (C, C), 1)
    incl = row >= col
    strict = row > col
    ltri = jnp.where(incl, 1.0, 0.0).astype(BF16)

    sm = sm_ref[0]
    lg_all = -jnp.exp(alog_ref[...]) * _softplus(sm + dtb_ref[...])
    lgh, lgl = _split(lg_all)
    b_all = _dot(ltri, lgh) + _dot(ltri, lgl)
    utri = jnp.where(row <= col, 1.0, 0.0).astype(BF16)
    b_all_t = _dot_tn(lgh, utri) + _dot_tn(lgl, utri)
    beta_all = _sigmoid(sm)
    z = z_ref[0]
    g = g_ref[...]
    for h in range(DN_HEADS):
        qs = slice(h * DN_DK, (h + 1) * DN_DK)
        ksl = slice(DN_HEADS * DN_DK + h * DN_DK, DN_HEADS * DN_DK + (h + 1) * DN_DK)
        vsl = slice(2 * DN_HEADS * DN_DK + h * DN_DV, 2 * DN_HEADS * DN_DK + (h + 1) * DN_DV)
        q = cs[:, qs]
        k = cs[:, ksl]
        v = cs[:, vsl]
        q = q * lax.rsqrt(jnp.sum(q * q, axis=-1, keepdims=True) + EPS) * (DN_DK ** -0.5)
        k = k * lax.rsqrt(jnp.sum(k * k, axis=-1, keepdims=True) + EPS)
        bcol = b_all[:, SM_DEC + h:SM_DEC + h + 1]
        brow = b_all_t[SM_DEC + h:SM_DEC + h + 1, :]
        beta = beta_all[:, SM_BETA + h:SM_BETA + h + 1]
        dec = jnp.where(incl, jnp.exp(jnp.where(incl, bcol - brow, 0.0)), 0.0)
        kb = k.astype(BF16)
        qb = q.astype(BF16)
        kk = _dot_nt(kb, kb)
        qk = _dot_nt(qb, kb) * dec
        p = jnp.where(strict, -(beta * dec * kk), 0.0)
        t_inv = _unit_lower_inverse(p, row, col, C)
        s = s_scr[h]
        sb = s.astype(BF16)
        eb = jnp.exp(bcol)
        rhs = beta * (v - eb * _dot(kb, sb))
        uu = _dot3(_split(t_inv), _split(rhs))
        ub = uu.astype(BF16)
        o = eb * _dot(qb, sb) + _dot(qk.astype(BF16), ub)
        blast = bcol[C - 1:C, :]
        kd = (k * jnp.exp(blast - bcol)).astype(BF16)
        s_scr[h] = jnp.exp(blast) * s + _dot_tn(kd, ub)
        zh = z[:, qs]
        mix_ref[0, :, qs] = _rms(o, g) * (zh * _sigmoid(zh))

    @pl.when(c == pl.num_programs(1) - 1)
    def _():
        s_out_ref[0] = s_scr[...]


def _gdn(p3, cw, hist8, alog, dtb, g, s0, C):
    B, L, _ = p3.shape
    nc = L // C
    blk = lambda w, cb: pl.BlockSpec((1, C, w), lambda b, c, cb=cb: (b, c, cb))
    full = lambda shp: pl.BlockSpec(shp, lambda b, c: (0,) * len(shp))
    st_spec = pl.BlockSpec((1, DN_HEADS, DN_DK, DN_DV), lambda b, c: (b, 0, 0, 0))
    hist_spec = pl.BlockSpec((1, SUBLANES, _DC), lambda b, c: (b, 0, 0))
    return pl.pallas_call(
        functools.partial(_gdn_kernel, C=C),
        grid=(B, nc),
        in_specs=[blk(_DC, COL_DC // _DC), blk(LANES, COL_SM // LANES), blk(_DW, COL_DZ // _DW),
                  full((CONV_W, _DC)), hist_spec, full((1, LANES)), full((1, LANES)), full((1, DN_DV)),
                  st_spec],
        out_specs=[pl.BlockSpec((1, C, _DW), lambda b, c: (b, c, 0)), st_spec, hist_spec],
        out_shape=[jax.ShapeDtypeStruct((B, L, _DW), F32),
                   jax.ShapeDtypeStruct((B, DN_HEADS, DN_DK, DN_DV), F32),
                   jax.ShapeDtypeStruct((B, SUBLANES, _DC), F32)],
        scratch_shapes=[pltpu.VMEM((DN_HEADS, DN_DK, DN_DV), F32),
                        pltpu.VMEM((C + SUBLANES, _DC), F32)],
        compiler_params=pltpu.CompilerParams(dimension_semantics=("parallel", "arbitrary"),
                                             vmem_limit_bytes=VMEM_LIMIT),
        name="gdn",
    )(p3, p3, p3, cw, hist8, alog, dtb, g, s0)


def _sb_kernel(q_ref, k_ref, v_ref, o_ref, acc, carry, *, tq, tk, q_start):
    i = pl.program_id(2)
    qmin = q_start + i * tq
    jm = qmin // tk
    lane = lax.broadcasted_iota(jnp.int32, (tq, LANES), 1)
    first = lane < SB_DH
    q = q_ref[0] * (SB_DH ** -0.5)
    qh = (jnp.where(first, q, 0.0).astype(BF16), jnp.where(first, 0.0, q).astype(BF16))
    kr = lax.broadcasted_iota(jnp.int32, (tk, tk), 0)
    kc = lax.broadcasted_iota(jnp.int32, (tk, tk), 1)
    utri = jnp.where(kr >= kc, 1.0, 0.0).astype(BF16)
    acc[...] = jnp.zeros_like(acc)
    carry[...] = jnp.zeros_like(carry)

    def block(j, masked):
        start = pl.multiple_of(j * tk, tk)
        kb = k_ref[0, pl.ds(start, tk), :].astype(BF16)
        vb = v_ref[0, pl.ds(start, tk), :].astype(BF16)
        if masked:
            qpos = qmin + lax.broadcasted_iota(jnp.int32, (tq, tk), 0)
            kpos = j * tk + lax.broadcasted_iota(jnp.int32, (tq, tk), 1)
            valid = kpos < qpos
        for h in range(2):
            zz = _dot_nt(qh[h], kb)
            sp = _softplus(zz)
            if masked:
                sp = jnp.where(valid, sp, 0.0)
            hi, lo = _split(sp)
            cs = _dot(hi, utri) + _dot(lo, utri)
            w = jnp.exp(zz - cs - carry[h])
            if masked:
                w = jnp.where(valid, w, 0.0)
            acc[h] += _dot(w.astype(BF16), vb)
            carry[h] += cs[:, 0:1]

    block(jm, True)

    def body(n, _):
        block(jm - 1 - n, False)
        return 0

    lax.fori_loop(0, jm, body, 0)
    o_ref[0] = jnp.where(first, acc[0], acc[1])


def _sb_attention(q_arr, q_cb, k_arr, k_cb, v_arr, v_cb, *, tq, tk, q_start, name):
    B, Tq = q_arr.shape[:2]
    Tk = k_arr.shape[1]
    assert Tq % tq == 0 and Tk % tk == 0 and tk % tq == 0 and q_start % tk == 0
    assert q_start + Tq <= Tk
    hp = SB_HEADS // 2
    kv_spec = lambda cb0: pl.BlockSpec((1, Tk, LANES), lambda b, h, i, cb0=cb0: (b, 0, cb0 + h))
    return pl.pallas_call(
        functools.partial(_sb_kernel, tq=tq, tk=tk, q_start=q_start),
        grid=(B, hp, Tq // tq),
        in_specs=[pl.BlockSpec((1, tq, LANES), lambda b, h, i: (b, i, q_cb + h)),
                  kv_spec(k_cb), kv_spec(v_cb)],
        out_specs=pl.BlockSpec((1, tq, LANES), lambda b, h, i: (b, i, h)),
        out_shape=jax.ShapeDtypeStruct((B, Tq, SB_HEADS * SB_DH), F32),
        scratch_shapes=[pltpu.VMEM((2, tq, LANES), F32), pltpu.VMEM((2, tq, 1), F32)],
        compiler_params=pltpu.CompilerParams(dimension_semantics=("parallel", "parallel", "arbitrary"),
                                             vmem_limit_bytes=VMEM_LIMIT),
        name=name,
    )(q_arr, k_arr, v_arr)


def _to_heads(t, n_heads):
    b, l, _ = t.shape
    return t.reshape(b, l, n_heads, -1).transpose(0, 2, 1, 3)


def _from_heads(t):
    b, h, l, d = t.shape
    return t.transpose(0, 2, 1, 3).reshape(b, l, h * d)


def _pad_lanes(vec, offset):
    out = jnp.zeros((1, LANES), F32)
    return lax.dynamic_update_slice(out, vec.reshape(1, -1).astype(F32), (0, offset))


def _group(x, s_gla, s_dn, conv_hist, cache_k, cache_v, C, tq, w):
    B, L, D = x.shape
    n = B * L
    x2 = x.reshape(n, D)
    tm = 512

    p = _linear([(x2, 0)], w["ab_w_in"], norm=w["ab_norm"], tm=256, name="ab_in")
    p3 = p.reshape(B, L, AB_COLS)
    mix_gla, s_gla_t = _gla(p3, w["gla_w_gate"], w["gla_b_gate"], w["gla_out_norm"],
                            jnp.swapaxes(s_gla, -1, -2), C)
    hist8 = jnp.pad(conv_hist, ((0, 0), (SUBLANES - (CONV_W - 1), 0), (0, 0)))
    mix_dn, s_dn_new, conv8 = _gdn(p3, w["dn_conv_w"], hist8, w["dn_a_log"], w["dn_dt_bias"],
                                   w["dn_out_norm"], s_dn, C)
    h1 = _linear([(mix_gla.reshape(n, _GV), 0), (mix_dn.reshape(n, _DW), 0)], w["ab_w_out"],
                 res=(x2, 0), tm=tm, name="ab_out")

    qkvz = _linear([(h1, 0)], w["sb_w_in"], norm=w["sb_norm"], tm=256, name="sb_in")
    W = SB_HEADS * SB_DH
    qkvz3 = qkvz.reshape(B, L, 4 * W)
    k_new = qkvz3[:, :, W:2 * W]
    v_new = qkvz3[:, :, 2 * W:3 * W]
    nb = W // LANES
    if cache_k is None:
        o = _sb_attention(qkvz3, 0, qkvz3, nb, qkvz3, 2 * nb, tq=tq, tk=tq, q_start=0, name="sb_prompt")
    else:
        past = cache_k.shape[2]
        tk = 256
        tk_pad = -(-(past + L) // tk) * tk
        pad = ((0, 0), (0, tk_pad - past - L), (0, 0))
        k_all = jnp.pad(jnp.concatenate([_from_heads(cache_k), k_new], axis=1), pad)
        v_all = jnp.pad(jnp.concatenate([_from_heads(cache_v), v_new], axis=1), pad)
        o = _sb_attention(qkvz3, 0, k_all, 0, v_all, 0, tq=tq, tk=tk, q_start=past, name="sb_sample")
    y = _linear([(o.reshape(n, W), 0)], w["sb_w_out"], gate=(qkvz, 3), res=(h1, 0),
                post=w["final_norm"], tm=tm, name="sb_out")
    return (y.reshape(B, L, D), jnp.swapaxes(s_gla_t, -1, -2), s_dn_new,
            conv8[:, SUBLANES - (CONV_W - 1):], _to_heads(k_new, SB_HEADS), _to_heads(v_new, SB_HEADS))


def kernel(x_prompt, x_sample, state_gla, state_delta, state_conv, cache_k, cache_v, ab_norm, ab_w_in,
           gla_w_gate, gla_b_gate, gla_out_norm, dn_conv_w, dn_a_log, dn_dt_bias, dn_out_norm, ab_w_out,
           sb_norm, sb_w_in, sb_w_out, final_norm):
    assert ab_w_in.shape[0] == 1 and sb_w_in.shape[0] == 1, "one (A|B) layer followed by one C layer"
    D = x_prompt.shape[-1]
    wi = ab_w_in[0]
    offs, acc = [], 0
    for s in (_GQ, _GQ, _GV, GLA_RANK, _GV, _DC, DN_HEADS, DN_HEADS, _DW):
        offs.append((acc, acc + s))
        acc += s
    gq, gk, gv, gr, gz, dqkv, dbeta, da, dz = [wi[:, a:b] for a, b in offs]
    small = jnp.concatenate([gr, dbeta, da], axis=1)
    w_ab = jnp.concatenate(
        [gq, gk, gv, gz, dqkv, dz, small,
         jnp.zeros((D, AB_COLS - COL_SM - small.shape[1]), wi.dtype)], axis=1).astype(BF16)
    wg = jnp.zeros((LANES, _GQ), F32).at[SM_RANK:SM_RANK + GLA_RANK].set(gla_w_gate[0]).astype(BF16)
    w = {
        "ab_norm": ab_norm[0], "ab_w_in": w_ab, "gla_w_gate": wg,
        "gla_b_gate": gla_b_gate[0].reshape(1, _GQ), "gla_out_norm": gla_out_norm[0].reshape(1, GLA_DV),
        "dn_conv_w": dn_conv_w[0], "dn_a_log": _pad_lanes(dn_a_log[0], SM_DEC),
        "dn_dt_bias": _pad_lanes(dn_dt_bias[0], SM_DEC), "dn_out_norm": dn_out_norm[0].reshape(1, DN_DV),
        "ab_w_out": ab_w_out[0].astype(BF16), "sb_norm": sb_norm[0], "sb_w_in": sb_w_in[0].astype(BF16),
        "sb_w_out": sb_w_out[0].astype(BF16), "final_norm": final_norm,
    }
    bp = x_prompt.shape[0]
    yp, gp, dp, cp, kp, vp = _group(
        x_prompt, jnp.zeros((bp, GLA_HEADS, GLA_DK, GLA_DV), F32), jnp.zeros((bp, DN_HEADS, DN_DK, DN_DV), F32),
        jnp.zeros((bp, CONV_W - 1, _DC), F32), None, None, CHUNK, 256, w)
    ys, gs, ds, cs, ks, vs = _group(
        x_sample, state_gla[0], state_delta[0], state_conv[0], cache_k[0], cache_v[0],
        x_sample.shape[1], x_sample.shape[1], w)
    st = lambda a: a[None]
    return (yp, ys, st(gp), st(gs), st(dp), st(ds), st(cp), st(cs), st(kp), st(ks), st(vp), st(vs))
```

```python
import functools

import jax
import jax.numpy as jnp
from jax import lax
from jax.experimental import pallas as pl
from jax.experimental.pallas import tpu as pltpu

F32 = jnp.float32
BF16 = jnp.bfloat16

EPS = 1e-6
CHUNK = 64
GLA_HEADS = 4
GLA_DK = 64
GLA_DV = 128
GLA_RANK = 16
GLA_TAU = 16.0
DN_HEADS = 4
DN_DK = 128
DN_DV = 128
CONV_W = 4
SB_HEADS = 16
SB_DH = 64

LANES = 128
SUBLANES = 8
INV_BLOCK = 16
SB_SLAB = 128
SB_NEG = -1e30
LOG2E = 1.4426950408889634
VMEM_LIMIT = 56 * 1024 * 1024

_GQ = GLA_HEADS * GLA_DK
_GV = GLA_HEADS * GLA_DV
_DC = 2 * DN_HEADS * DN_DK + DN_HEADS * DN_DV
_DW = DN_HEADS * DN_DV
COL_QK = 0
COL_GV = 2 * _GQ
COL_GZ = COL_GV + _GV
COL_DC = COL_GZ + _GV
COL_DZ = COL_DC + _DC
COL_SM = COL_DZ + _DW
AB_COLS = 3840
SM_RANK = 0
SM_BETA = GLA_RANK
SM_DEC = GLA_RANK + DN_HEADS


def _sigmoid(x):
    return 1.0 / (1.0 + jnp.exp(-x))


def _softplus(x):
    return jnp.maximum(x, 0.0) + jnp.log(1.0 + jnp.exp(-jnp.abs(x)))


def _split(a):
    hi = a.astype(BF16)
    lo = (a - hi.astype(F32)).astype(BF16)
    return hi, lo


def _dot(a, b):
    return jnp.dot(a, b, preferred_element_type=F32)


def _dot_nt(a, b):
    return lax.dot_general(a, b, (((1,), (1,)), ((), ())), preferred_element_type=F32)


def _dot_tn(a, b):
    return lax.dot_general(a, b, (((0,), (0,)), ((), ())), preferred_element_type=F32)


def _dot3(a, b):
    ah, al = a
    bh, bl = b
    return _dot(ah, bh) + _dot(ah, bl) + _dot(al, bh)


def _rms(x, g):
    return x * lax.rsqrt(jnp.mean(x * x, axis=-1, keepdims=True) + EPS) * g


def _linear_kernel(*refs, n_x, has_norm, has_gate, has_res, has_post):
    refs = list(refs)
    xs = [refs.pop(0) for _ in range(n_x)]
    g_ref = refs.pop(0) if has_norm else None
    z_ref = refs.pop(0) if has_gate else None
    w_ref = refs.pop(0)
    r_ref = refs.pop(0) if has_res else None
    p_ref = refs.pop(0) if has_post else None
    o_ref = refs.pop(0)
    x = xs[0][...] if n_x == 1 else jnp.concatenate([r[...] for r in xs], axis=-1)
    if has_norm:
        x = _rms(x, g_ref[...])
    if has_gate:
        z = z_ref[...]
        x = x * (z * _sigmoid(z))
    y = _dot(x.astype(BF16), w_ref[...])
    if has_res:
        y = y + r_ref[...]
    if has_post:
        y = _rms(y, p_ref[...])
    o_ref[...] = y


def _linear(xs, w, *, norm=None, gate=None, res=None, post=None, tm, name):
    n = xs[0][0].shape[0]
    k, m = w.shape
    tm = min(tm, n)
    assert n % tm == 0
    in_specs, args = [], []
    kx = k // len(xs)
    for arr, cb in xs:
        in_specs.append(pl.BlockSpec((tm, kx), lambda i, cb=cb: (i, cb)))
        args.append(arr)
    if norm is not None:
        in_specs.append(pl.BlockSpec((1, k), lambda i: (0, 0)))
        args.append(norm.reshape(1, k))
    if gate is not None:
        in_specs.append(pl.BlockSpec((tm, k), lambda i, cb=gate[1]: (i, cb)))
        args.append(gate[0])
    in_specs.append(pl.BlockSpec((k, m), lambda i: (0, 0)))
    args.append(w)
    if res is not None:
        in_specs.append(pl.BlockSpec((tm, m), lambda i, cb=res[1]: (i, cb)))
        args.append(res[0])
    if post is not None:
        in_specs.append(pl.BlockSpec((1, m), lambda i: (0, 0)))
        args.append(post.reshape(1, m))
    kern = functools.partial(_linear_kernel, n_x=len(xs), has_norm=norm is not None,
                             has_gate=gate is not None, has_res=res is not None,
                             has_post=post is not None)
    return pl.pallas_call(
        kern,
        grid=(n // tm,),
        in_specs=in_specs,
        out_specs=pl.BlockSpec((tm, m), lambda i: (i, 0)),
        out_shape=jax.ShapeDtypeStruct((n, m), F32),
        compiler_params=pltpu.CompilerParams(dimension_semantics=("parallel",),
                                             vmem_limit_bytes=VMEM_LIMIT),
        name=name,
    )(*args)


def _gla_kernel(qk_ref, v_ref, z_ref, sm_ref, wg_ref, bg_ref, g_ref, s0_ref,
                mix_ref, s_out_ref, s_scr, *, C):
    c = pl.program_id(1)

    @pl.when(c == 0)
    def _():
        s_scr[...] = s0_ref[0]

    row = lax.broadcasted_iota(jnp.int32, (C, C), 0)
    col = lax.broadcasted_iota(jnp.int32, (C, C), 1)
    incl = row >= col
    ltri = jnp.where(incl, 1.0, 0.0).astype(BF16)

    sm = sm_ref[0]
    x = _dot(sm.astype(BF16), wg_ref[...]) + bg_ref[...]
    lg = -_softplus(-x) * (1.0 / GLA_TAU)
    lgh, lgl = _split(lg)
    b = _dot(ltri, lgh) + _dot(ltri, lgl)
    mid = C // 2 - 1
    bmid = b[mid:mid + 1, :]
    blast = b[C - 1:C, :]
    qk = qk_ref[0]
    q = qk[:, :_GQ] * (GLA_DK ** -0.5)
    k = qk[:, _GQ:]
    qe = (q * jnp.exp(b - bmid)).astype(BF16)
    ke = (k * jnp.exp(bmid - b)).astype(BF16)
    qd = (q * jnp.exp(b)).astype(BF16)
    kd = (k * jnp.exp(blast - b)).astype(BF16)
    dlast = jnp.exp(blast)
    v = v_ref[0]
    z = z_ref[0]
    g = g_ref[...]
    H = range(GLA_HEADS)
    ks = [slice(h * GLA_DK, (h + 1) * GLA_DK) for h in H]
    vs = [slice(h * GLA_DV, (h + 1) * GLA_DV) for h in H]
    a = [jnp.where(incl, _dot_nt(qe[:, ks[h]], ke[:, ks[h]]), 0.0).astype(BF16) for h in H]
    st = [s_scr[h] for h in H]
    vh = [v[:, vs[h]].astype(BF16) for h in H]
    oi = [_dot_nt(qd[:, ks[h]], st[h].astype(BF16)) for h in H]
    sn = [st[h] * dlast[:, ks[h]] + _dot_tn(vh[h], kd[:, ks[h]]) for h in H]
    o = [_dot(a[h], vh[h]) + oi[h] for h in H]
    for h in H:
        s_scr[h] = sn[h]
        zh = z[:, vs[h]]
        mix_ref[0, :, vs[h]] = _rms(o[h], g) * (zh * _sigmoid(zh))

    @pl.when(c == pl.num_programs(1) - 1)
    def _():
        s_out_ref[0] = s_scr[...]


def _gla(p3, wg, bg, g, s0t, C):
    B, L, _ = p3.shape
    nc = L // C
    blk = lambda w, cb: pl.BlockSpec((1, C, w), lambda b, c, cb=cb: (b, c, cb))
    full = lambda shp: pl.BlockSpec(shp, lambda b, c: (0,) * len(shp))
    st_spec = pl.BlockSpec((1, GLA_HEADS, GLA_DV, GLA_DK), lambda b, c: (b, 0, 0, 0))
    return pl.pallas_call(
        functools.partial(_gla_kernel, C=C),
        grid=(B, nc),
        in_specs=[blk(2 * _GQ, COL_QK // (2 * _GQ)), blk(_GV, COL_GV // _GV), blk(_GV, COL_GZ // _GV),
                  blk(LANES, COL_SM // LANES), full((LANES, _GQ)), full((1, _GQ)), full((1, GLA_DV)),
                  st_spec],
        out_specs=[pl.BlockSpec((1, C, _GV), lambda b, c: (b, c, 0)), st_spec],
        out_shape=[jax.ShapeDtypeStruct((B, L, _GV), F32),
                   jax.ShapeDtypeStruct((B, GLA_HEADS, GLA_DV, GLA_DK), F32)],
        scratch_shapes=[pltpu.VMEM((GLA_HEADS, GLA_DV, GLA_DK), F32)],
        compiler_params=pltpu.CompilerParams(dimension_semantics=("parallel", "arbitrary"),
                                             vmem_limit_bytes=VMEM_LIMIT),
        name="gla",
    )(p3, p3, p3, p3, wg, bg, g, s0t)


def _unit_lower_inverse(ps, row, col, C):
    eye = jnp.where(row == col, 1.0, 0.0)
    nb = C // INV_BLOCK
    if nb > 1:
        same = (row // INV_BLOCK) == (col // INV_BLOCK)
        pds = [jnp.where(same, p, 0.0) for p in ps]
    else:
        pds = ps
    xs = [eye + pd for pd in pds]
    pws = [_split(pd) for pd in pds]
    steps = INV_BLOCK.bit_length() - 1
    for _ in range(1, steps):
        pws = [_split(_dot3(pw, pw)) for pw in pws]
        xs = [x + _dot3(pw, _split(x)) for pw, x in zip(pws, xs)]
    if nb == 1:
        return xs
    assert nb <= 4
    xss = [_split(x) for x in xs]
    n1s = [_split(_dot3(xh, _split(p - pd))) for xh, p, pd in zip(xss, ps, pds)]
    n2s = [_split(_dot3(n1, n1)) for n1 in n1s]
    ys = [x + _dot3(n1, xh) for x, n1, xh in zip(xs, n1s, xss)]
    return [y + _dot3(n2, _split(y)) for y, n2 in zip(ys, n2s)]


def _gdn_kernel(u_ref, sm_ref, z_ref, cw_ref, hist_ref, alog_ref, dtb_ref, g_ref, s0_ref,
                mix_ref, s_out_ref, conv_out_ref, s_scr, ubuf, *, C):
    c = pl.program_id(1)
    T8 = SUBLANES

    @pl.when(c == 0)
    def _():
        s_scr[...] = s0_ref[0]
        ubuf[0:T8, :] = hist_ref[0]

    u = u_ref[0]
    ubuf[T8:T8 + C, :] = u
    cw = cw_ref[...]
    conv = u * cw[CONV_W - 1:CONV_W, :]
    for i in range(CONV_W - 1):
        off = T8 - (CONV_W - 1) + i
        conv = conv + ubuf[off:off + C, :] * cw[i:i + 1, :]
    cs = conv * _sigmoid(conv)
    tail = u[C - T8:, :]
    ubuf[0:T8, :] = tail

    @pl.when(c == pl.num_programs(1) - 1)
    def _():
        conv_out_ref[0] = tail

    row = lax.broadcasted_iota(jnp.int32, (C, C), 0)
    col = lax.broadcasted_iota(jnp.int32, (C, C), 1)
    incl = row >= col
    strict = row > col
    ltri = jnp.where(incl, 1.0, 0.0).astype(BF16)

    sm = sm_ref[0]
    lg_all = -jnp.exp(alog_ref[...]) * _softplus(sm + dtb_ref[...])
    lgh, lgl = _split(lg_all)
    b_all = _dot(ltri, lgh) + _dot(ltri, lgl)
    utri = jnp.where(row <= col, 1.0, 0.0).astype(BF16)
    b_all_t = _dot_tn(lgh, utri) + _dot_tn(lgl, utri)
    beta_all = _sigmoid(sm)
    z = z_ref[0]
    g = g_ref[...]
    H = range(DN_HEADS)
    hs = [slice(h * DN_DK, (h + 1) * DN_DK) for h in H]
    qs = [cs[:, hs[h]] for h in H]
    ks = [cs[:, DN_HEADS * DN_DK + h * DN_DK:DN_HEADS * DN_DK + (h + 1) * DN_DK] for h in H]
    vs = [cs[:, 2 * DN_HEADS * DN_DK + h * DN_DV:2 * DN_HEADS * DN_DK + (h + 1) * DN_DV] for h in H]
    qs = [q * lax.rsqrt(jnp.sum(q * q, axis=-1, keepdims=True) + EPS) * (DN_DK ** -0.5) for q in qs]
    ks = [k * lax.rsqrt(jnp.sum(k * k, axis=-1, keepdims=True) + EPS) for k in ks]
    bcols = [b_all[:, SM_DEC + h:SM_DEC + h + 1] for h in H]
    brows = [b_all_t[SM_DEC + h:SM_DEC + h + 1, :] for h in H]
    betas = [beta_all[:, SM_BETA + h:SM_BETA + h + 1] for h in H]
    decs = [jnp.where(incl, jnp.exp(jnp.where(incl, bc - br, 0.0)), 0.0) for bc, br in zip(bcols, brows)]
    kbs = [k.astype(BF16) for k in ks]
    qbs = [q.astype(BF16) for q in qs]
    kks = [_dot_nt(kb, kb) for kb in kbs]
    ss = [s_scr[h] for h in H]
    sbs = [s.astype(BF16) for s in ss]
    kss = [_dot(kb, sb) for kb, sb in zip(kbs, sbs)]
    qss = [_dot(qb, sb) for qb, sb in zip(qbs, sbs)]
    qks = [(_dot_nt(qb, kb) * dec).astype(BF16) for qb, kb, dec in zip(qbs, kbs, decs)]
    ps = [jnp.where(strict, -(beta * dec * kk), 0.0) for beta, dec, kk in zip(betas, decs, kks)]
    t_invs = _unit_lower_inverse(ps, row, col, C)
    ebs = [jnp.exp(bc) for bc in bcols]
    rhss = [beta * (v - eb * ksv) for beta, v, eb, ksv in zip(betas, vs, ebs, kss)]
    ubs = [_dot3(_split(t), _split(r)).astype(BF16) for t, r in zip(t_invs, rhss)]
    os_ = [eb * qsv + _dot(qk, ub) for eb, qsv, qk, ub in zip(ebs, qss, qks, ubs)]
    blasts = [bc[C - 1:C, :] for bc in bcols]
    kds = [(k * jnp.exp(bl - bc)).astype(BF16) for k, bl, bc in zip(ks, blasts, bcols)]
    s_new = [jnp.exp(bl) * s + _dot_tn(kd, ub) for bl, s, kd, ub in zip(blasts, ss, kds, ubs)]
    for h in H:
        s_scr[h] = s_new[h]
        zh = z[:, hs[h]]
        mix_ref[0, :, hs[h]] = _rms(os_[h], g) * (zh * _sigmoid(zh))

    @pl.when(c == pl.num_programs(1) - 1)
    def _():
        s_out_ref[0] = s_scr[...]


def _gdn(p3, cw, hist8, alog, dtb, g, s0, C):
    B, L, _ = p3.shape
    nc = L // C
    blk = lambda w, cb: pl.BlockSpec((1, C, w), lambda b, c, cb=cb: (b, c, cb))
    full = lambda shp: pl.BlockSpec(shp, lambda b, c: (0,) * len(shp))
    st_spec = pl.BlockSpec((1, DN_HEADS, DN_DK, DN_DV), lambda b, c: (b, 0, 0, 0))
    hist_spec = pl.BlockSpec((1, SUBLANES, _DC), lambda b, c: (b, 0, 0))
    return pl.pallas_call(
        functools.partial(_gdn_kernel, C=C),
        grid=(B, nc),
        in_specs=[blk(_DC, COL_DC // _DC), blk(LANES, COL_SM // LANES), blk(_DW, COL_DZ // _DW),
                  full((CONV_W, _DC)), hist_spec, full((1, LANES)), full((1, LANES)), full((1, DN_DV)),
                  st_spec],
        out_specs=[pl.BlockSpec((1, C, _DW), lambda b, c: (b, c, 0)), st_spec, hist_spec],
        out_shape=[jax.ShapeDtypeStruct((B, L, _DW), F32),
                   jax.ShapeDtypeStruct((B, DN_HEADS, DN_DK, DN_DV), F32),
                   jax.ShapeDtypeStruct((B, SUBLANES, _DC), F32)],
        scratch_shapes=[pltpu.VMEM((DN_HEADS, DN_DK, DN_DV), F32),
                        pltpu.VMEM((C + SUBLANES, _DC), F32)],
        compiler_params=pltpu.CompilerParams(dimension_semantics=("parallel", "arbitrary"),
                                             vmem_limit_bytes=VMEM_LIMIT),
        name="gdn",
    )(p3, p3, p3, cw, hist8, alog, dtb, g, s0)


def _sb_kernel(q_ref, k_ref, v_ref, o_ref, acc, carry, zbuf, zbuf2, ebuf, *, tq, tk, q_start, slab):
    i = pl.program_id(2)
    qmin = q_start + i * tq
    jm = qmin // tk
    lane = lax.broadcasted_iota(jnp.int32, (tq, LANES), 1)
    first = lane < SB_DH
    q = q_ref[0] * (SB_DH ** -0.5 * LOG2E)
    q2 = jnp.concatenate([jnp.where(first, q, 0.0), jnp.where(first, 0.0, q)], axis=0).astype(BF16)
    kr = lax.broadcasted_iota(jnp.int32, (tk, tk), 0)
    kc = lax.broadcasted_iota(jnp.int32, (tk, tk), 1)
    utri = jnp.where(kr >= kc, 1.0, 0.0).astype(BF16)
    n = 2 * tq // slab
    rows = [slice(t * slab, (t + 1) * slab) for t in range(n)]

    def keys(ref, p):
        j = jnp.clip(jm - p, 0, jm)
        return ref[0, pl.ds(pl.multiple_of(j * tk, tk), tk), :].astype(BF16)

    acc[...] = jnp.zeros_like(acc)
    carry[...] = jnp.zeros_like(carry)
    ebuf[...] = jnp.full((2 * tq, tk), SB_NEG, F32)
    kb0 = keys(k_ref, 0)
    for t in range(n):
        qpos = qmin + (t * slab + lax.broadcasted_iota(jnp.int32, (slab, tk), 0)) % tq
        kpos = jm * tk + lax.broadcasted_iota(jnp.int32, (slab, tk), 1)
        z0 = _dot_nt(q2[rows[t]], kb0) + jnp.where(kpos < qpos, 0.0, SB_NEG)
        zbuf[rows[t], :] = z0
        zbuf2[rows[t], :] = z0

    def body(it, _):
        kb = keys(k_ref, it + 1)
        vb = keys(v_ref, it - 1)
        for r in rows:
            w = jnp.exp2(ebuf[r, :])
            acc[r, :] += _dot(w.astype(BF16), vb)
            zz = zbuf[r, :]
            neg_abs = lax.bitcast_convert_type(
                lax.bitcast_convert_type(zz, jnp.int32) | jnp.int32(-2 ** 31), F32)
            sp = jnp.maximum(zz, 0.0) + jnp.log2(1.0 + jnp.exp2(neg_abs))
            cs = _dot(sp.astype(BF16), utri)
            z_next = _dot_nt(q2[r], kb)
            ebuf[r, :] = zbuf2[r, :] - cs - carry[r, :]
            carry[r, :] += cs[:, 0:1]
            zbuf[r, :] = z_next
            zbuf2[r, :] = z_next
        return 0

    lax.fori_loop(0, jm + 2, body, 0)
    o_ref[0] = jnp.where(first, acc[0:tq, :], acc[tq:2 * tq, :])


def _sb_attention(q_arr, q_cb, k_arr, k_cb, v_arr, v_cb, *, tq, tk, q_start, name):
    B, Tq = q_arr.shape[:2]
    Tk = k_arr.shape[1]
    assert Tq % tq == 0 and Tk % tk == 0 and tk % tq == 0 and q_start % tk == 0
    assert q_start + Tq <= Tk
    hp = SB_HEADS // 2
    kv_spec = lambda cb0: pl.BlockSpec((1, Tk, LANES), lambda b, h, i, cb0=cb0: (b, 0, cb0 + h))
    slab = min(SB_SLAB, 2 * tq)
    return pl.pallas_call(
        functools.partial(_sb_kernel, tq=tq, tk=tk, q_start=q_start, slab=slab),
        grid=(B, hp, Tq // tq),
        in_specs=[pl.BlockSpec((1, tq, LANES), lambda b, h, i: (b, i, q_cb + h)),
                  kv_spec(k_cb), kv_spec(v_cb)],
        out_specs=pl.BlockSpec((1, tq, LANES), lambda b, h, i: (b, i, h)),
        out_shape=jax.ShapeDtypeStruct((B, Tq, SB_HEADS * SB_DH), F32),
        scratch_shapes=[pltpu.VMEM((2 * tq, LANES), F32), pltpu.VMEM((2 * tq, 1), F32),
                        pltpu.VMEM((2 * tq, tk), F32), pltpu.VMEM((2 * tq, tk), F32),
                        pltpu.VMEM((2 * tq, tk), F32)],
        compiler_params=pltpu.CompilerParams(dimension_semantics=("parallel", "parallel", "arbitrary"),
                                             vmem_limit_bytes=VMEM_LIMIT),
        name=name,
    )(q_arr, k_arr, v_arr)


def _to_heads(t, n_heads):
    b, l, _ = t.shape
    return t.reshape(b, l, n_heads, -1).transpose(0, 2, 1, 3)


def _from_heads(t):
    b, h, l, d = t.shape
    return t.transpose(0, 2, 1, 3).reshape(b, l, h * d)


def _pad_lanes(vec, offset):
    out = jnp.zeros((1, LANES), F32)
    return lax.dynamic_update_slice(out, vec.reshape(1, -1).astype(F32), (0, offset))


def _group(x, s_gla, s_dn, conv_hist, cache_k, cache_v, C, tq, w):
    B, L, D = x.shape
    n = B * L
    x2 = x.reshape(n, D)
    tm = 512

    p = _linear([(x2, 0)], w["ab_w_in"], norm=w["ab_norm"], tm=256, name="ab_in")
    p3 = p.reshape(B, L, AB_COLS)
    mix_gla, s_gla_t = _gla(p3, w["gla_w_gate"], w["gla_b_gate"], w["gla_out_norm"],
                            jnp.swapaxes(s_gla, -1, -2), C)
    hist8 = jnp.pad(conv_hist, ((0, 0), (SUBLANES - (CONV_W - 1), 0), (0, 0)))
    mix_dn, s_dn_new, conv8 = _gdn(p3, w["dn_conv_w"], hist8, w["dn_a_log"], w["dn_dt_bias"],
                                   w["dn_out_norm"], s_dn, C)
    h1 = _linear([(mix_gla.reshape(n, _GV), 0), (mix_dn.reshape(n, _DW), 0)], w["ab_w_out"],
                 res=(x2, 0), tm=tm, name="ab_out")

    qkvz = _linear([(h1, 0)], w["sb_w_in"], norm=w["sb_norm"], tm=256, name="sb_in")
    W = SB_HEADS * SB_DH
    qkvz3 = qkvz.reshape(B, L, 4 * W)
    k_new = qkvz3[:, :, W:2 * W]
    v_new = qkvz3[:, :, 2 * W:3 * W]
    nb = W // LANES
    if cache_k is None:
        o = _sb_attention(qkvz3, 0, qkvz3, nb, qkvz3, 2 * nb, tq=tq, tk=tq, q_start=0, name="sb_prompt")
    else:
        past = cache_k.shape[2]
        tk = 256
        tk_pad = -(-(past + L) // tk) * tk
        pad = ((0, 0), (0, tk_pad - past - L), (0, 0))
        k_all = jnp.pad(jnp.concatenate([_from_heads(cache_k), k_new], axis=1), pad)
        v_all = jnp.pad(jnp.concatenate([_from_heads(cache_v), v_new], axis=1), pad)
        o = _sb_attention(qkvz3, 0, k_all, 0, v_all, 0, tq=tq, tk=tk, q_start=past, name="sb_sample")
    y = _linear([(o.reshape(n, W), 0)], w["sb_w_out"], gate=(qkvz, 3), res=(h1, 0),
                post=w["final_norm"], tm=tm, name="sb_out")
    return (y.reshape(B, L, D), jnp.swapaxes(s_gla_t, -1, -2), s_dn_new,
            conv8[:, SUBLANES - (CONV_W - 1):], _to_heads(k_new, SB_HEADS), _to_heads(v_new, SB_HEADS))


def kernel(x_prompt, x_sample, state_gla, state_delta, state_conv, cache_k, cache_v, ab_norm, ab_w_in,
           gla_w_gate, gla_b_gate, gla_out_norm, dn_conv_w, dn_a_log, dn_dt_bias, dn_out_norm, ab_w_out,
           sb_norm, sb_w_in, sb_w_out, final_norm):
    assert ab_w_in.shape[0] == 1 and sb_w_in.shape[0] == 1, "one (A|B) layer followed by one C layer"
    D = x_prompt.shape[-1]
    wi = ab_w_in[0]
    offs, acc = [], 0
    for s in (_GQ, _GQ, _GV, GLA_RANK, _GV, _DC, DN_HEADS, DN_HEADS, _DW):
        offs.append((acc, acc + s))
        acc += s
    gq, gk, gv, gr, gz, dqkv, dbeta, da, dz = [wi[:, a:b] for a, b in offs]
    small = jnp.concatenate([gr, dbeta, da], axis=1)
    w_ab = jnp.concatenate(
        [gq, gk, gv, gz, dqkv, dz, small,
         jnp.zeros((D, AB_COLS - COL_SM - small.shape[1]), wi.dtype)], axis=1).astype(BF16)
    wg = jnp.zeros((LANES, _GQ), F32).at[SM_RANK:SM_RANK + GLA_RANK].set(gla_w_gate[0]).astype(BF16)
    w = {
        "ab_norm": ab_norm[0], "ab_w_in": w_ab, "gla_w_gate": wg,
        "gla_b_gate": gla_b_gate[0].reshape(1, _GQ), "gla_out_norm": gla_out_norm[0].reshape(1, GLA_DV),
        "dn_conv_w": dn_conv_w[0], "dn_a_log": _pad_lanes(dn_a_log[0], SM_DEC),
        "dn_dt_bias": _pad_lanes(dn_dt_bias[0], SM_DEC), "dn_out_norm": dn_out_norm[0].reshape(1, DN_DV),
        "ab_w_out": ab_w_out[0].astype(BF16), "sb_norm": sb_norm[0], "sb_w_in": sb_w_in[0].astype(BF16),
        "sb_w_out": sb_w_out[0].astype(BF16), "final_norm": final_norm,
    }
    bp = x_prompt.shape[0]
    yp, gp, dp, cp, kp, vp = _group(
        x_prompt, jnp.zeros((bp, GLA_HEADS, GLA_DK, GLA_DV), F32), jnp.zeros((bp, DN_HEADS, DN_DK, DN_DV), F32),
        jnp.zeros((bp, CONV_W - 1, _DC), F32), None, None, CHUNK, 256, w)
    ys, gs, ds, cs, ks, vs = _group(
        x_sample, state_gla[0], state_delta[0], state_conv[0], cache_k[0], cache_v[0],
        x_sample.shape[1], x_sample.shape[1], w)
    st = lambda a: a[None]
    return (yp, ys, st(gp), st(gs), st(dp), st(ds), st(cp), st(cs), st(kp), st(ks), st(vp), st(vs))
```

```python
import functools

import jax
import jax.numpy as jnp
from jax import lax
from jax.experimental import pallas as pl
from jax.experimental.pallas import tpu as pltpu

F32 = jnp.float32
BF16 = jnp.bfloat16

EPS = 1e-6
CHUNK = 64
GLA_HEADS = 4
GLA_DK = 64
GLA_DV = 128
GLA_RANK = 16
GLA_TAU = 16.0
DN_HEADS = 4
DN_DK = 128
DN_DV = 128
CONV_W = 4
SB_HEADS = 16
SB_DH = 64

LANES = 128
SUBLANES = 8
INV_BLOCK = 16
AB_GROUP = 4
SB_SLAB = 128
SB_NEG = -1e30
LOG2E = 1.4426950408889634
VMEM_LIMIT = 56 * 1024 * 1024

_GQ = GLA_HEADS * GLA_DK
_GV = GLA_HEADS * GLA_DV
_DC = 2 * DN_HEADS * DN_DK + DN_HEADS * DN_DV
_DW = DN_HEADS * DN_DV
COL_QK = 0
COL_GV = 2 * _GQ
COL_GZ = COL_GV + _GV
COL_DC = COL_GZ + _GV
COL_DZ = COL_DC + _DC
COL_SM = COL_DZ + _DW
AB_COLS = 3840
SM_RANK = 0
SM_BETA = GLA_RANK
SM_DEC = GLA_RANK + DN_HEADS


def _sigmoid(x):
    return 1.0 / (1.0 + jnp.exp(-x))


def _softplus(x):
    return jnp.maximum(x, 0.0) + jnp.log(1.0 + jnp.exp(-jnp.abs(x)))


def _split(a):
    hi = a.astype(BF16)
    lo = (a - hi.astype(F32)).astype(BF16)
    return hi, lo


def _dot(a, b):
    return jnp.dot(a, b, preferred_element_type=F32)


def _dot_nt(a, b):
    return lax.dot_general(a, b, (((1,), (1,)), ((), ())), preferred_element_type=F32)


def _dot_tn(a, b):
    return lax.dot_general(a, b, (((0,), (0,)), ((), ())), preferred_element_type=F32)


def _dot3(a, b):
    ah, al = a
    bh, bl = b
    return _dot(ah, bh) + _dot(ah, bl) + _dot(al, bh)


def _rms(x, g):
    return x * lax.rsqrt(jnp.mean(x * x, axis=-1, keepdims=True) + EPS) * g


def _linear_kernel(*refs, n_x, has_norm, has_gate, has_res, has_post):
    refs = list(refs)
    xs = [refs.pop(0) for _ in range(n_x)]
    g_ref = refs.pop(0) if has_norm else None
    z_ref = refs.pop(0) if has_gate else None
    w_ref = refs.pop(0)
    r_ref = refs.pop(0) if has_res else None
    p_ref = refs.pop(0) if has_post else None
    o_ref = refs.pop(0)
    x = xs[0][...] if n_x == 1 else jnp.concatenate([r[...] for r in xs], axis=-1)
    if has_norm:
        x = _rms(x, g_ref[...])
    if has_gate:
        z = z_ref[...]
        x = x * (z * _sigmoid(z))
    y = _dot(x.astype(BF16), w_ref[...])
    if has_res:
        y = y + r_ref[...]
    if has_post:
        y = _rms(y, p_ref[...])
    o_ref[...] = y


def _linear(xs, w, *, norm=None, gate=None, res=None, post=None, tm, name):
    n = xs[0][0].shape[0]
    k, m = w.shape
    tm = min(tm, n)
    assert n % tm == 0
    in_specs, args = [], []
    kx = k // len(xs)
    for arr, cb in xs:
        in_specs.append(pl.BlockSpec((tm, kx), lambda i, cb=cb: (i, cb)))
        args.append(arr)
    if norm is not None:
        in_specs.append(pl.BlockSpec((1, k), lambda i: (0, 0)))
        args.append(norm.reshape(1, k))
    if gate is not None:
        in_specs.append(pl.BlockSpec((tm, k), lambda i, cb=gate[1]: (i, cb)))
        args.append(gate[0])
    in_specs.append(pl.BlockSpec((k, m), lambda i: (0, 0)))
    args.append(w)
    if res is not None:
        in_specs.append(pl.BlockSpec((tm, m), lambda i, cb=res[1]: (i, cb)))
        args.append(res[0])
    if post is not None:
        in_specs.append(pl.BlockSpec((1, m), lambda i: (0, 0)))
        args.append(post.reshape(1, m))
    kern = functools.partial(_linear_kernel, n_x=len(xs), has_norm=norm is not None,
                             has_gate=gate is not None, has_res=res is not None,
                             has_post=post is not None)
    return pl.pallas_call(
        kern,
        grid=(n // tm,),
        in_specs=in_specs,
        out_specs=pl.BlockSpec((tm, m), lambda i: (i, 0)),
        out_shape=jax.ShapeDtypeStruct((n, m), F32),
        compiler_params=pltpu.CompilerParams(dimension_semantics=("parallel",),
                                             vmem_limit_bytes=VMEM_LIMIT),
        name=name,
    )(*args)


def _sb_in_kernel(x_ref, g_ref, w_ref, qkv_ref, z_ref, kh_ref, vh_ref):
    W = SB_HEADS * SB_DH
    y = _dot(_rms(x_ref[...], g_ref[...]).astype(BF16), w_ref[...])
    k = y[:, W:2 * W]
    v = y[:, 2 * W:3 * W]
    qkv_ref[:, 0:W] = (y[:, 0:W] * (SB_DH ** -0.5 * LOG2E)).astype(BF16)
    qkv_ref[:, W:2 * W] = k.astype(BF16)
    qkv_ref[:, 2 * W:3 * W] = v.astype(BF16)
    z_ref[...] = y[:, 3 * W:]
    for h in range(SB_HEADS):
        kh_ref[0, h] = k[:, h * SB_DH:(h + 1) * SB_DH]
        vh_ref[0, h] = v[:, h * SB_DH:(h + 1) * SB_DH]


def _sb_in(h1, g, w, B, L, tm):
    n, d = h1.shape
    W = SB_HEADS * SB_DH
    tm = min(tm, L)
    nl = L // tm
    head_spec = pl.BlockSpec((1, SB_HEADS, tm, SB_DH), lambda i: (i // nl, 0, i % nl, 0))
    return pl.pallas_call(
        _sb_in_kernel,
        grid=(n // tm,),
        in_specs=[pl.BlockSpec((tm, d), lambda i: (i, 0)), pl.BlockSpec((1, d), lambda i: (0, 0)),
                  pl.BlockSpec((d, 4 * W), lambda i: (0, 0))],
        out_specs=[pl.BlockSpec((tm, 3 * W), lambda i: (i, 0)), pl.BlockSpec((tm, W), lambda i: (i, 0)),
                   head_spec, head_spec],
        out_shape=[jax.ShapeDtypeStruct((n, 3 * W), BF16), jax.ShapeDtypeStruct((n, W), F32),
                   jax.ShapeDtypeStruct((B, SB_HEADS, L, SB_DH), F32),
                   jax.ShapeDtypeStruct((B, SB_HEADS, L, SB_DH), F32)],
        compiler_params=pltpu.CompilerParams(dimension_semantics=("parallel",),
                                             vmem_limit_bytes=VMEM_LIMIT),
        name="sb_in",
    )(h1, g.reshape(1, d), w)


def _gla_kernel(qk_ref, v_ref, z_ref, sm_ref, wg_ref, bg_ref, g_ref, s0_ref,
                mix_ref, s_out_ref, s_scr, *, C):
    c = pl.program_id(1)

    @pl.when(c == 0)
    def _():
        s_scr[...] = s0_ref[0]

    row = lax.broadcasted_iota(jnp.int32, (C, C), 0)
    col = lax.broadcasted_iota(jnp.int32, (C, C), 1)
    incl = row >= col
    ltri = jnp.where(incl, 1.0, 0.0).astype(BF16)

    sm = sm_ref[0]
    x = _dot(sm.astype(BF16), wg_ref[...]) + bg_ref[...]
    lg = -_softplus(-x) * (1.0 / GLA_TAU)
    lgh, lgl = _split(lg)
    b = _dot(ltri, lgh) + _dot(ltri, lgl)
    mid = C // 2 - 1
    bmid = b[mid:mid + 1, :]
    blast = b[C - 1:C, :]
    qk = qk_ref[0]
    q = qk[:, :_GQ] * (GLA_DK ** -0.5)
    k = qk[:, _GQ:]
    qe = (q * jnp.exp(b - bmid)).astype(BF16)
    ke = (k * jnp.exp(bmid - b)).astype(BF16)
    qd = (q * jnp.exp(b)).astype(BF16)
    kd = (k * jnp.exp(blast - b)).astype(BF16)
    dlast = jnp.exp(blast)
    v = v_ref[0]
    z = z_ref[0]
    g = g_ref[...]
    H = range(GLA_HEADS)
    ks = [slice(h * GLA_DK, (h + 1) * GLA_DK) for h in H]
    vs = [slice(h * GLA_DV, (h + 1) * GLA_DV) for h in H]
    a = [jnp.where(incl, _dot_nt(qe[:, ks[h]], ke[:, ks[h]]), 0.0).astype(BF16) for h in H]
    st = [s_scr[h] for h in H]
    vh = [v[:, vs[h]].astype(BF16) for h in H]
    oi = [_dot_nt(qd[:, ks[h]], st[h].astype(BF16)) for h in H]
    sn = [st[h] * dlast[:, ks[h]] + _dot_tn(vh[h], kd[:, ks[h]]) for h in H]
    o = [_dot(a[h], vh[h]) + oi[h] for h in H]
    for h in H:
        s_scr[h] = sn[h]
        zh = z[:, vs[h]]
        mix_ref[0, :, vs[h]] = _rms(o[h], g) * (zh * _sigmoid(zh))

    @pl.when(c == pl.num_programs(1) - 1)
    def _():
        s_out_ref[0] = s_scr[...]


def _gla(p3, wg, bg, g, s0t, C):
    B, L, _ = p3.shape
    nc = L // C
    blk = lambda w, cb: pl.BlockSpec((1, C, w), lambda b, c, cb=cb: (b, c, cb))
    full = lambda shp: pl.BlockSpec(shp, lambda b, c: (0,) * len(shp))
    st_spec = pl.BlockSpec((1, GLA_HEADS, GLA_DV, GLA_DK), lambda b, c: (b, 0, 0, 0))
    return pl.pallas_call(
        functools.partial(_gla_kernel, C=C),
        grid=(B, nc),
        in_specs=[blk(2 * _GQ, COL_QK // (2 * _GQ)), blk(_GV, COL_GV // _GV), blk(_GV, COL_GZ // _GV),
                  blk(LANES, COL_SM // LANES), full((LANES, _GQ)), full((1, _GQ)), full((1, GLA_DV)),
                  st_spec],
        out_specs=[pl.BlockSpec((1, C, _GV), lambda b, c: (b, c, 0)), st_spec],
        out_shape=[jax.ShapeDtypeStruct((B, L, _GV), F32),
                   jax.ShapeDtypeStruct((B, GLA_HEADS, GLA_DV, GLA_DK), F32)],
        scratch_shapes=[pltpu.VMEM((GLA_HEADS, GLA_DV, GLA_DK), F32)],
        compiler_params=pltpu.CompilerParams(dimension_semantics=("parallel", "arbitrary"),
                                             vmem_limit_bytes=VMEM_LIMIT),
        name="gla",
    )(p3, p3, p3, p3, wg, bg, g, s0t)


def _unit_lower_inverse(ps, row, col, C):
    eye = jnp.where(row == col, 1.0, 0.0)
    nb = C // INV_BLOCK
    if nb > 1:
        same = (row // INV_BLOCK) == (col // INV_BLOCK)
        pds = [jnp.where(same, p, 0.0) for p in ps]
    else:
        pds = ps
    xs = [eye + pd for pd in pds]
    pws = [_split(pd) for pd in pds]
    steps = INV_BLOCK.bit_length() - 1
    for _ in range(1, steps):
        pws = [_split(_dot3(pw, pw)) for pw in pws]
        xs = [x + _dot3(pw, _split(x)) for pw, x in zip(pws, xs)]
    if nb == 1:
        return xs
    assert nb <= 4
    xss = [_split(x) for x in xs]
    n1s = [_split(_dot3(xh, _split(p - pd))) for xh, p, pd in zip(xss, ps, pds)]
    n2s = [_split(_dot3(n1, n1)) for n1 in n1s]
    ys = [x + _dot3(n1, xh) for x, n1, xh in zip(xs, n1s, xss)]
    return [y + _dot3(n2, _split(y)) for y, n2 in zip(ys, n2s)]


def _gdn_kernel(u_ref, sm_ref, z_ref, cw_ref, hist_ref, alog_ref, dtb_ref, g_ref, s0_ref,
                mix_ref, s_out_ref, conv_out_ref, s_scr, ubuf, *, C):
    c = pl.program_id(1)
    T8 = SUBLANES

    @pl.when(c == 0)
    def _():
        s_scr[...] = s0_ref[0]
        ubuf[0:T8, :] = hist_ref[0]

    u = u_ref[0]
    ubuf[T8:T8 + C, :] = u
    cw = cw_ref[...]
    conv = u * cw[CONV_W - 1:CONV_W, :]
    for i in range(CONV_W - 1):
        off = T8 - (CONV_W - 1) + i
        conv = conv + ubuf[off:off + C, :] * cw[i:i + 1, :]
    cs = conv * _sigmoid(conv)
    tail = u[C - T8:, :]
    ubuf[0:T8, :] = tail

    @pl.when(c == pl.num_programs(1) - 1)
    def _():
        conv_out_ref[0] = tail

    row = lax.broadcasted_iota(jnp.int32, (C, C), 0)
    col = lax.broadcasted_iota(jnp.int32, (C, C), 1)
    incl = row >= col
    strict = row > col
    ltri = jnp.where(incl, 1.0, 0.0).astype(BF16)

    sm = sm_ref[0]
    lg_all = -jnp.exp(alog_ref[...]) * _softplus(sm + dtb_ref[...])
    lgh, lgl = _split(lg_all)
    b_all = _dot(ltri, lgh) + _dot(ltri, lgl)
    utri = jnp.where(row <= col, 1.0, 0.0).astype(BF16)
    b_all_t = _dot_tn(lgh, utri) + _dot_tn(lgl, utri)
    beta_all = _sigmoid(sm)
    z = z_ref[0]
    g = g_ref[...]
    H = range(DN_HEADS)
    hs = [slice(h * DN_DK, (h + 1) * DN_DK) for h in H]
    qs = [cs[:, hs[h]] for h in H]
    ks = [cs[:, DN_HEADS * DN_DK + h * DN_DK:DN_HEADS * DN_DK + (h + 1) * DN_DK] for h in H]
    vs = [cs[:, 2 * DN_HEADS * DN_DK + h * DN_DV:2 * DN_HEADS * DN_DK + (h + 1) * DN_DV] for h in H]
    qs = [q * lax.rsqrt(jnp.sum(q * q, axis=-1, keepdims=True) + EPS) * (DN_DK ** -0.5) for q in qs]
    ks = [k * lax.rsqrt(jnp.sum(k * k, axis=-1, keepdims=True) + EPS) for k in ks]
    bcols = [b_all[:, SM_DEC + h:SM_DEC + h + 1] for h in H]
    brows = [b_all_t[SM_DEC + h:SM_DEC + h + 1, :] for h in H]
    betas = [beta_all[:, SM_BETA + h:SM_BETA + h + 1] for h in H]
    decs = [jnp.where(incl, jnp.exp(jnp.where(incl, bc - br, 0.0)), 0.0) for bc, br in zip(bcols, brows)]
    kbs = [k.astype(BF16) for k in ks]
    qbs = [q.astype(BF16) for q in qs]
    kks = [_dot_nt(kb, kb) for kb in kbs]
    ss = [s_scr[h] for h in H]
    sbs = [s.astype(BF16) for s in ss]
    kss = [_dot(kb, sb) for kb, sb in zip(kbs, sbs)]
    qss = [_dot(qb, sb) for qb, sb in zip(qbs, sbs)]
    qks = [(_dot_nt(qb, kb) * dec).astype(BF16) for qb, kb, dec in zip(qbs, kbs, decs)]
    ps = [jnp.where(strict, -(beta * dec * kk), 0.0) for beta, dec, kk in zip(betas, decs, kks)]
    t_invs = _unit_lower_inverse(ps, row, col, C)
    ebs = [jnp.exp(bc) for bc in bcols]
    rhss = [beta * (v - eb * ksv) for beta, v, eb, ksv in zip(betas, vs, ebs, kss)]
    ubs = [_dot3(_split(t), _split(r)).astype(BF16) for t, r in zip(t_invs, rhss)]
    os_ = [eb * qsv + _dot(qk, ub) for eb, qsv, qk, ub in zip(ebs, qss, qks, ubs)]
    blasts = [bc[C - 1:C, :] for bc in bcols]
    kds = [(k * jnp.exp(bl - bc)).astype(BF16) for k, bl, bc in zip(ks, blasts, bcols)]
    s_new = [jnp.exp(bl) * s + _dot_tn(kd, ub) for bl, s, kd, ub in zip(blasts, ss, kds, ubs)]
    for h in H:
        s_scr[h] = s_new[h]
        zh = z[:, hs[h]]
        mix_ref[0, :, hs[h]] = _rms(os_[h], g) * (zh * _sigmoid(zh))

    @pl.when(c == pl.num_programs(1) - 1)
    def _():
        s_out_ref[0] = s_scr[...]


def _gdn(p3, cw, hist8, alog, dtb, g, s0, C):
    B, L, _ = p3.shape
    nc = L // C
    blk = lambda w, cb: pl.BlockSpec((1, C, w), lambda b, c, cb=cb: (b, c, cb))
    full = lambda shp: pl.BlockSpec(shp, lambda b, c: (0,) * len(shp))
    st_spec = pl.BlockSpec((1, DN_HEADS, DN_DK, DN_DV), lambda b, c: (b, 0, 0, 0))
    hist_spec = pl.BlockSpec((1, SUBLANES, _DC), lambda b, c: (b, 0, 0))
    return pl.pallas_call(
        functools.partial(_gdn_kernel, C=C),
        grid=(B, nc),
        in_specs=[blk(_DC, COL_DC // _DC), blk(LANES, COL_SM // LANES), blk(_DW, COL_DZ // _DW),
                  full((CONV_W, _DC)), hist_spec, full((1, LANES)), full((1, LANES)), full((1, DN_DV)),
                  st_spec],
        out_specs=[pl.BlockSpec((1, C, _DW), lambda b, c: (b, c, 0)), st_spec, hist_spec],
        out_shape=[jax.ShapeDtypeStruct((B, L, _DW), F32),
                   jax.ShapeDtypeStruct((B, DN_HEADS, DN_DK, DN_DV), F32),
                   jax.ShapeDtypeStruct((B, SUBLANES, _DC), F32)],
        scratch_shapes=[pltpu.VMEM((DN_HEADS, DN_DK, DN_DV), F32),
                        pltpu.VMEM((C + SUBLANES, _DC), F32)],
        compiler_params=pltpu.CompilerParams(dimension_semantics=("parallel", "arbitrary"),
                                             vmem_limit_bytes=VMEM_LIMIT),
        name="gdn",
    )(p3, p3, p3, cw, hist8, alog, dtb, g, s0)


def _ab_kernel(qk_ref, gv_ref, gz_ref, u_ref, dz_ref, sm_ref, wg_ref, bg_ref, gg_ref, cw_ref, hist_ref,
               alog_ref, dtb_ref, gd_ref, sg0_ref, sd0_ref,
               mix_ref, sg_out_ref, sd_out_ref, conv_out_ref, sg_scr, sd_scr, ubuf, *, C, G):
    c = pl.program_id(1)
    R = G * C
    T8 = SUBLANES
    HG = range(GLA_HEADS)
    HD = range(DN_HEADS)
    GS = range(G)
    rs = [slice(g * C, (g + 1) * C) for g in GS]

    @pl.when(c == 0)
    def _():
        sg_scr[...] = sg0_ref[0]
        sd_scr[...] = sd0_ref[0]
        ubuf[0:T8, :] = hist_ref[0]

    u = u_ref[0]
    ubuf[T8:T8 + R, :] = u
    cw = cw_ref[...]
    conv = u * cw[CONV_W - 1:CONV_W, :]
    for i in range(CONV_W - 1):
        off = T8 - (CONV_W - 1) + i
        conv = conv + ubuf[off:off + R, :] * cw[i:i + 1, :]
    cs = conv * _sigmoid(conv)
    tail = u[R - T8:, :]
    ubuf[0:T8, :] = tail

    @pl.when(c == pl.num_programs(1) - 1)
    def _():
        conv_out_ref[0] = tail

    rr = lax.broadcasted_iota(jnp.int32, (R, R), 0)
    rc = lax.broadcasted_iota(jnp.int32, (R, R), 1)
    same = (rr // C) == (rc // C)
    lblk = jnp.where(jnp.logical_and(same, rr >= rc), 1.0, 0.0).astype(BF16)
    ublk = jnp.where(jnp.logical_and(same, rr <= rc), 1.0, 0.0).astype(BF16)
    row = lax.broadcasted_iota(jnp.int32, (C, C), 0)
    col = lax.broadcasted_iota(jnp.int32, (C, C), 1)
    incl = row >= col
    strict = row > col

    sm = sm_ref[0]
    x = _dot(sm.astype(BF16), wg_ref[...]) + bg_ref[...]
    lgh, lgl = _split(-_softplus(-x) * (1.0 / GLA_TAU))
    b = _dot(lblk, lgh) + _dot(lblk, lgl)
    dh, dl = _split(-jnp.exp(alog_ref[...]) * _softplus(sm + dtb_ref[...]))
    b_all = _dot(lblk, dh) + _dot(lblk, dl)
    b_all_t = _dot_tn(dh, ublk) + _dot_tn(dl, ublk)
    beta_all = _sigmoid(sm)

    qk = qk_ref[0]
    q = qk[:, :_GQ] * (GLA_DK ** -0.5)
    k = qk[:, _GQ:]
    mid = C // 2 - 1
    qe, ke, qd, kd, dlast = [], [], [], [], []
    for g in GS:
        bg_ = b[rs[g], :]
        bmid = bg_[mid:mid + 1, :]
        blast = bg_[C - 1:C, :]
        qe.append((q[rs[g], :] * jnp.exp(bg_ - bmid)).astype(BF16))
        ke.append((k[rs[g], :] * jnp.exp(bmid - bg_)).astype(BF16))
        qd.append((q[rs[g], :] * jnp.exp(bg_)).astype(BF16))
        kd.append((k[rs[g], :] * jnp.exp(blast - bg_)).astype(BF16))
        dlast.append(jnp.exp(blast))
    gks = [slice(h * GLA_DK, (h + 1) * GLA_DK) for h in HG]
    gvs = [slice(h * GLA_DV, (h + 1) * GLA_DV) for h in HG]
    gv = gv_ref[0].astype(BF16)
    att = [[jnp.where(incl, _dot_nt(qe[g][:, gks[h]], ke[g][:, gks[h]]), 0.0).astype(BF16) for h in HG]
           for g in GS]

    dhs = [slice(h * DN_DK, (h + 1) * DN_DK) for h in HD]
    nq = DN_HEADS * DN_DK
    qn = [cs[:, dhs[h]] for h in HD]
    kn = [cs[:, nq + h * DN_DK:nq + (h + 1) * DN_DK] for h in HD]
    vn = [cs[:, 2 * nq + h * DN_DV:2 * nq + (h + 1) * DN_DV] for h in HD]
    qn = [t * lax.rsqrt(jnp.sum(t * t, axis=-1, keepdims=True) + EPS) * (DN_DK ** -0.5) for t in qn]
    kn = [t * lax.rsqrt(jnp.sum(t * t, axis=-1, keepdims=True) + EPS) for t in kn]
    qb = [t.astype(BF16) for t in qn]
    kb = [t.astype(BF16) for t in kn]
    GH = [(g, h) for g in GS for h in HD]
    bcol = {(g, h): b_all[rs[g], SM_DEC + h:SM_DEC + h + 1] for g, h in GH}
    brow = {(g, h): b_all_t[SM_DEC + h:SM_DEC + h + 1, rs[g]] for g, h in GH}
    beta = {(g, h): beta_all[rs[g], SM_BETA + h:SM_BETA + h + 1] for g, h in GH}
    dec = {gh: jnp.where(incl, jnp.exp(jnp.where(incl, bcol[gh] - brow[gh], 0.0)), 0.0) for gh in GH}
    kk = {(g, h): _dot_nt(kb[h][rs[g], :], kb[h][rs[g], :]) for g, h in GH}
    qkd = {(g, h): (_dot_nt(qb[h][rs[g], :], kb[h][rs[g], :]) * dec[(g, h)]).astype(BF16) for g, h in GH}
    ps = [jnp.where(strict, -(beta[gh] * dec[gh] * kk[gh]), 0.0) for gh in GH]
    t_inv = dict(zip(GH, [_split(t) for t in _unit_lower_inverse(ps, row, col, C)]))
    eb = {gh: jnp.exp(bcol[gh]) for gh in GH}
    blast_d = {gh: bcol[gh][C - 1:C, :] for gh in GH}
    kdd = {(g, h): (kn[h][rs[g], :] * jnp.exp(blast_d[(g, h)] - bcol[(g, h)])).astype(BF16) for g, h in GH}

    gz = gz_ref[0]
    dz = dz_ref[0]
    gg = gg_ref[...]
    gd = gd_ref[...]
    for g in GS:
        r = rs[g]
        sd = [sd_scr[h] for h in HD]
        sdb = [s.astype(BF16) for s in sd]
        st = [sg_scr[h] for h in HG]
        ksd = [_dot(kb[h][r, :], sdb[h]) for h in HD]
        qsd = [_dot(qb[h][r, :], sdb[h]) for h in HD]
        gvh = [gv[r, gvs[h]] for h in HG]
        oi = [_dot_nt(qd[g][:, gks[h]], st[h].astype(BF16)) for h in HG]
        sg_new = [st[h] * dlast[g][:, gks[h]] + _dot_tn(gvh[h], kd[g][:, gks[h]]) for h in HG]
        og = [_dot(att[g][h], gvh[h]) + oi[h] for h in HG]
        rhs = [beta[(g, h)] * (vn[h][r, :] - eb[(g, h)] * ksd[h]) for h in HD]
        ub = [_dot3(t_inv[(g, h)], _split(rhs[h])).astype(BF16) for h in HD]
        od = [eb[(g, h)] * qsd[h] + _dot(qkd[(g, h)], ub[h]) for h in HD]
        sd_new = [jnp.exp(blast_d[(g, h)]) * sd[h] + _dot_tn(kdd[(g, h)], ub[h]) for h in HD]
        for h in HG:
            sg_scr[h] = sg_new[h]
            zh = gz[r, gvs[h]]
            mix_ref[0, r, gvs[h]] = _rms(og[h], gg) * (zh * _sigmoid(zh))
        for h in HD:
            sd_scr[h] = sd_new[h]
            zh = dz[r, dhs[h]]
            mix_ref[0, r, _GV + h * DN_DV:_GV + (h + 1) * DN_DV] = _rms(od[h], gd) * (zh * _sigmoid(zh))

    @pl.when(c == pl.num_programs(1) - 1)
    def _():
        sg_out_ref[0] = sg_scr[...]
        sd_out_ref[0] = sd_scr[...]


def _ab_mix(p3, wg, bg, gg, cw, hist8, alog, dtb, gd, sg0t, sd0, C, G):
    B, L, _ = p3.shape
    R = G * C
    assert L % R == 0
    blk = lambda w, cb: pl.BlockSpec((1, R, w), lambda b, c, cb=cb: (b, c, cb))
    full = lambda shp: pl.BlockSpec(shp, lambda b, c: (0,) * len(shp))
    sg_spec = pl.BlockSpec((1, GLA_HEADS, GLA_DV, GLA_DK), lambda b, c: (b, 0, 0, 0))
    sd_spec = pl.BlockSpec((1, DN_HEADS, DN_DK, DN_DV), lambda b, c: (b, 0, 0, 0))
    hist_spec = pl.BlockSpec((1, SUBLANES, _DC), lambda b, c: (b, 0, 0))
    return pl.pallas_call(
        functools.partial(_ab_kernel, C=C, G=G),
        grid=(B, L // R),
        in_specs=[blk(2 * _GQ, COL_QK // (2 * _GQ)), blk(_GV, COL_GV // _GV), blk(_GV, COL_GZ // _GV),
                  blk(_DC, COL_DC // _DC), blk(_DW, COL_DZ // _DW), blk(LANES, COL_SM // LANES),
                  full((LANES, _GQ)), full((1, _GQ)), full((1, GLA_DV)), full((CONV_W, _DC)), hist_spec,
                  full((1, LANES)), full((1, LANES)), full((1, DN_DV)), sg_spec, sd_spec],
        out_specs=[pl.BlockSpec((1, R, _GV + _DW), lambda b, c: (b, c, 0)), sg_spec, sd_spec, hist_spec],
        out_shape=[jax.ShapeDtypeStruct((B, L, _GV + _DW), F32),
                   jax.ShapeDtypeStruct((B, GLA_HEADS, GLA_DV, GLA_DK), F32),
                   jax.ShapeDtypeStruct((B, DN_HEADS, DN_DK, DN_DV), F32),
                   jax.ShapeDtypeStruct((B, SUBLANES, _DC), F32)],
        scratch_shapes=[pltpu.VMEM((GLA_HEADS, GLA_DV, GLA_DK), F32),
                        pltpu.VMEM((DN_HEADS, DN_DK, DN_DV), F32),
                        pltpu.VMEM((R + SUBLANES, _DC), F32)],
        compiler_params=pltpu.CompilerParams(dimension_semantics=("parallel", "arbitrary"),
                                             vmem_limit_bytes=VMEM_LIMIT),
        name="ab_mix",
    )(p3, p3, p3, p3, p3, p3, wg, bg, gg, cw, hist8, alog, dtb, gd, sg0t, sd0)


def _sb_kernel(q_ref, k_ref, v_ref, o_ref, acc, carry, zbuf, zbuf2, ebuf, *, tq, tk, q_start, slab):
    i = pl.program_id(2)
    qmin = q_start + i * tq
    jm = qmin // tk
    lane = lax.broadcasted_iota(jnp.int32, (tq, LANES), 1)
    first = lane < SB_DH
    q = q_ref[0]
    zero = jnp.zeros_like(q)
    q2 = jnp.concatenate([jnp.where(first, q, zero), jnp.where(first, zero, q)], axis=0)
    kr = lax.broadcasted_iota(jnp.int32, (tk, tk), 0)
    kc = lax.broadcasted_iota(jnp.int32, (tk, tk), 1)
    utri = jnp.where(kr >= kc, 1.0, 0.0).astype(BF16)
    n = 2 * tq // slab
    rows = [slice(t * slab, (t + 1) * slab) for t in range(n)]

    def keys(ref, p):
        j = jnp.clip(jm - p, 0, jm)
        return ref[0, pl.ds(pl.multiple_of(j * tk, tk), tk), :]

    acc[...] = jnp.zeros_like(acc)
    carry[...] = jnp.zeros_like(carry)
    ebuf[...] = jnp.full((2 * tq, tk), SB_NEG, F32)
    kb0 = keys(k_ref, 0)
    for t in range(n):
        qpos = qmin + (t * slab + lax.broadcasted_iota(jnp.int32, (slab, tk), 0)) % tq
        kpos = jm * tk + lax.broadcasted_iota(jnp.int32, (slab, tk), 1)
        z0 = _dot_nt(q2[rows[t]], kb0) + jnp.where(kpos < qpos, 0.0, SB_NEG)
        zbuf[rows[t], :] = z0
        zbuf2[rows[t], :] = z0

    def body(it, _):
        kb = keys(k_ref, it + 1)
        vb = keys(v_ref, it - 1)
        for r in rows:
            w = jnp.exp2(ebuf[r, :])
            acc[r, :] += _dot(w.astype(BF16), vb)
            zz = zbuf[r, :]
            neg_abs = lax.bitcast_convert_type(
                lax.bitcast_convert_type(zz, jnp.int32) | jnp.int32(-2 ** 31), F32)
            sp = jnp.maximum(zz, 0.0) + jnp.log2(1.0 + jnp.exp2(neg_abs))
            cs = _dot(sp.astype(BF16), utri)
            z_next = _dot_nt(q2[r], kb)
            ebuf[r, :] = zbuf2[r, :] - cs - carry[r, :]
            carry[r, :] += cs[:, 0:1]
            zbuf[r, :] = z_next
            zbuf2[r, :] = z_next
        return 0

    lax.fori_loop(0, jm + 2, body, 0)
    o_ref[0] = jnp.where(first, acc[0:tq, :], acc[tq:2 * tq, :])


def _sb_attention(q_arr, q_cb, k_arr, k_cb, v_arr, v_cb, *, tq, tk, q_start, name):
    B, Tq = q_arr.shape[:2]
    Tk = k_arr.shape[1]
    assert Tq % tq == 0 and Tk % tk == 0 and tk % tq == 0 and q_start % tk == 0
    assert q_start + Tq <= Tk
    hp = SB_HEADS // 2
    kv_spec = lambda cb0: pl.BlockSpec((1, Tk, LANES), lambda b, h, i, cb0=cb0: (b, 0, cb0 + h))
    slab = min(SB_SLAB, 2 * tq)
    return pl.pallas_call(
        functools.partial(_sb_kernel, tq=tq, tk=tk, q_start=q_start, slab=slab),
        grid=(B, hp, Tq // tq),
        in_specs=[pl.BlockSpec((1, tq, LANES), lambda b, h, i: (b, i, q_cb + h)),
                  kv_spec(k_cb), kv_spec(v_cb)],
        out_specs=pl.BlockSpec((1, tq, LANES), lambda b, h, i: (b, i, h)),
        out_shape=jax.ShapeDtypeStruct((B, Tq, SB_HEADS * SB_DH), F32),
        scratch_shapes=[pltpu.VMEM((2 * tq, LANES), F32), pltpu.VMEM((2 * tq, 1), F32),
                        pltpu.VMEM((2 * tq, tk), F32), pltpu.VMEM((2 * tq, tk), F32),
                        pltpu.VMEM((2 * tq, tk), F32)],
        compiler_params=pltpu.CompilerParams(dimension_semantics=("parallel", "parallel", "arbitrary"),
                                             vmem_limit_bytes=VMEM_LIMIT),
        name=name,
    )(q_arr, k_arr, v_arr)


def _to_heads(t, n_heads):
    b, l, _ = t.shape
    return t.reshape(b, l, n_heads, -1).transpose(0, 2, 1, 3)


def _from_heads(t):
    b, h, l, d = t.shape
    return t.transpose(0, 2, 1, 3).reshape(b, l, h * d)


def _pad_lanes(vec, offset):
    out = jnp.zeros((1, LANES), F32)
    return lax.dynamic_update_slice(out, vec.reshape(1, -1).astype(F32), (0, offset))


def _group(x, s_gla, s_dn, conv_hist, cache_k, cache_v, C, tq, w):
    B, L, D = x.shape
    n = B * L
    x2 = x.reshape(n, D)
    tm = 512

    p = _linear([(x2, 0)], w["ab_w_in"], norm=w["ab_norm"], tm=256, name="ab_in")
    p3 = p.reshape(B, L, AB_COLS)
    hist8 = jnp.pad(conv_hist, ((0, 0), (SUBLANES - (CONV_W - 1), 0), (0, 0)))
    mix, s_gla_t, s_dn_new, conv8 = _ab_mix(
        p3, w["gla_w_gate"], w["gla_b_gate"], w["gla_out_norm"], w["dn_conv_w"], hist8, w["dn_a_log"],
        w["dn_dt_bias"], w["dn_out_norm"], jnp.swapaxes(s_gla, -1, -2), s_dn, C, min(AB_GROUP, L // C))
    h1 = _linear([(mix.reshape(n, _GV + _DW), 0)], w["ab_w_out"], res=(x2, 0), tm=tm, name="ab_out")

    W = SB_HEADS * SB_DH
    qkv, z, k_heads, v_heads = _sb_in(h1, w["sb_norm"], w["sb_w_in"], B, L, 256)
    qkv3 = qkv.reshape(B, L, 3 * W)
    nb = W // LANES
    if cache_k is None:
        o = _sb_attention(qkv3, 0, qkv3, nb, qkv3, 2 * nb, tq=tq, tk=tq, q_start=0, name="sb_prompt")
    else:
        past = cache_k.shape[2]
        tk = 256
        tk_pad = -(-(past + L) // tk) * tk
        pad = ((0, 0), (0, tk_pad - past - L), (0, 0))
        k_all = jnp.pad(jnp.concatenate([_from_heads(cache_k).astype(BF16), qkv3[:, :, W:2 * W]], axis=1), pad)
        v_all = jnp.pad(jnp.concatenate([_from_heads(cache_v).astype(BF16), qkv3[:, :, 2 * W:]], axis=1), pad)
        o = _sb_attention(qkv3, 0, k_all, 0, v_all, 0, tq=tq, tk=tk, q_start=past, name="sb_sample")
    y = _linear([(o.reshape(n, W), 0)], w["sb_w_out"], gate=(z, 0), res=(h1, 0),
                post=w["final_norm"], tm=tm, name="sb_out")
    return (y.reshape(B, L, D), jnp.swapaxes(s_gla_t, -1, -2), s_dn_new,
            conv8[:, SUBLANES - (CONV_W - 1):], k_heads, v_heads)


def kernel(x_prompt, x_sample, state_gla, state_delta, state_conv, cache_k, cache_v, ab_norm, ab_w_in,
           gla_w_gate, gla_b_gate, gla_out_norm, dn_conv_w, dn_a_log, dn_dt_bias, dn_out_norm, ab_w_out,
           sb_norm, sb_w_in, sb_w_out, final_norm):
    assert ab_w_in.shape[0] == 1 and sb_w_in.shape[0] == 1, "one (A|B) layer followed by one C layer"
    D = x_prompt.shape[-1]
    wi = ab_w_in[0]
    offs, acc = [], 0
    for s in (_GQ, _GQ, _GV, GLA_RANK, _GV, _DC, DN_HEADS, DN_HEADS, _DW):
        offs.append((acc, acc + s))
        acc += s
    gq, gk, gv, gr, gz, dqkv, dbeta, da, dz = [wi[:, a:b] for a, b in offs]
    small = jnp.concatenate([gr, dbeta, da], axis=1)
    w_ab = jnp.concatenate(
        [gq, gk, gv, gz, dqkv, dz, small,
         jnp.zeros((D, AB_COLS - COL_SM - small.shape[1]), wi.dtype)], axis=1).astype(BF16)
    wg = jnp.zeros((LANES, _GQ), F32).at[SM_RANK:SM_RANK + GLA_RANK].set(gla_w_gate[0]).astype(BF16)
    w = {
        "ab_norm": ab_norm[0], "ab_w_in": w_ab, "gla_w_gate": wg,
        "gla_b_gate": gla_b_gate[0].reshape(1, _GQ), "gla_out_norm": gla_out_norm[0].reshape(1, GLA_DV),
        "dn_conv_w": dn_conv_w[0], "dn_a_log": _pad_lanes(dn_a_log[0], SM_DEC),
        "dn_dt_bias": _pad_lanes(dn_dt_bias[0], SM_DEC), "dn_out_norm": dn_out_norm[0].reshape(1, DN_DV),
        "ab_w_out": ab_w_out[0].astype(BF16), "sb_norm": sb_norm[0], "sb_w_in": sb_w_in[0].astype(BF16),
        "sb_w_out": sb_w_out[0].astype(BF16), "final_norm": final_norm,
    }
    bp = x_prompt.shape[0]
    yp, gp, dp, cp, kp, vp = _group(
        x_prompt, jnp.zeros((bp, GLA_HEADS, GLA_DK, GLA_DV), F32), jnp.zeros((bp, DN_HEADS, DN_DK, DN_DV), F32),
        jnp.zeros((bp, CONV_W - 1, _DC), F32), None, None, CHUNK, 256, w)
    ys, gs, ds, cs, ks, vs = _group(
        x_sample, state_gla[0], state_delta[0], state_conv[0], cache_k[0], cache_v[0],
        x_sample.shape[1], x_sample.shape[1], w)
    st = lambda a: a[None]
    return (yp, ys, st(gp), st(gs), st(dp), st(ds), st(cp), st(cs), st(kp), st(ks), st(vp), st(vs))
```

```python
import functools

import jax
import jax.numpy as jnp
from jax import lax
from jax.experimental import pallas as pl
from jax.experimental.pallas import tpu as pltpu

F32 = jnp.float32
BF16 = jnp.bfloat16

EPS = 1e-6
CHUNK = 64
GLA_HEADS = 4
GLA_DK = 64
GLA_DV = 128
GLA_RANK = 16
GLA_TAU = 16.0
DN_HEADS = 4
DN_DK = 128
DN_DV = 128
CONV_W = 4
SB_HEADS = 16
SB_DH = 64

LANES = 128
SUBLANES = 8
INV_BLOCK = 16
AB_GROUP = 4
SB_SLAB = 128
SB_NEG = -1e30
SB_DEAD = -160.0
SB_NORM_SLACK = 1.001
LOG2E = 1.4426950408889634
VMEM_LIMIT = 56 * 1024 * 1024

_GQ = GLA_HEADS * GLA_DK
_GV = GLA_HEADS * GLA_DV
_DC = 2 * DN_HEADS * DN_DK + DN_HEADS * DN_DV
_DW = DN_HEADS * DN_DV
COL_QK = 0
COL_GV = 2 * _GQ
COL_GZ = COL_GV + _GV
COL_DC = COL_GZ + _GV
COL_DZ = COL_DC + _DC
COL_SM = COL_DZ + _DW
AB_COLS = 3840
SM_RANK = 0
SM_BETA = GLA_RANK
SM_DEC = GLA_RANK + DN_HEADS


def _sigmoid(x):
    return 1.0 / (1.0 + jnp.exp(-x))


def _softplus(x):
    return jnp.maximum(x, 0.0) + jnp.log(1.0 + jnp.exp(-jnp.abs(x)))


def _split(a):
    hi = a.astype(BF16)
    lo = (a - hi.astype(F32)).astype(BF16)
    return hi, lo


def _dot(a, b):
    return jnp.dot(a, b, preferred_element_type=F32)


def _dot_nt(a, b):
    return lax.dot_general(a, b, (((1,), (1,)), ((), ())), preferred_element_type=F32)


def _dot_tn(a, b):
    return lax.dot_general(a, b, (((0,), (0,)), ((), ())), preferred_element_type=F32)


def _dot3(a, b):
    ah, al = a
    bh, bl = b
    return _dot(ah, bh) + _dot(ah, bl) + _dot(al, bh)


def _rms(x, g):
    return x * lax.rsqrt(jnp.mean(x * x, axis=-1, keepdims=True) + EPS) * g


def _linear_kernel(*refs, n_x, has_norm, has_gate, has_res, has_post):
    refs = list(refs)
    xs = [refs.pop(0) for _ in range(n_x)]
    g_ref = refs.pop(0) if has_norm else None
    z_ref = refs.pop(0) if has_gate else None
    w_ref = refs.pop(0)
    r_ref = refs.pop(0) if has_res else None
    p_ref = refs.pop(0) if has_post else None
    o_ref = refs.pop(0)
    x = xs[0][...] if n_x == 1 else jnp.concatenate([r[...] for r in xs], axis=-1)
    if has_norm:
        x = _rms(x, g_ref[...])
    if has_gate:
        z = z_ref[...]
        x = x * (z * _sigmoid(z))
    y = _dot(x.astype(BF16), w_ref[...])
    if has_res:
        y = y + r_ref[...]
    if has_post:
        y = _rms(y, p_ref[...])
    o_ref[...] = y


def _linear(xs, w, *, norm=None, gate=None, res=None, post=None, tm, name):
    n = xs[0][0].shape[0]
    k, m = w.shape
    tm = min(tm, n)
    assert n % tm == 0
    in_specs, args = [], []
    kx = k // len(xs)
    for arr, cb in xs:
        in_specs.append(pl.BlockSpec((tm, kx), lambda i, cb=cb: (i, cb)))
        args.append(arr)
    if norm is not None:
        in_specs.append(pl.BlockSpec((1, k), lambda i: (0, 0)))
        args.append(norm.reshape(1, k))
    if gate is not None:
        in_specs.append(pl.BlockSpec((tm, k), lambda i, cb=gate[1]: (i, cb)))
        args.append(gate[0])
    in_specs.append(pl.BlockSpec((k, m), lambda i: (0, 0)))
    args.append(w)
    if res is not None:
        in_specs.append(pl.BlockSpec((tm, m), lambda i, cb=res[1]: (i, cb)))
        args.append(res[0])
    if post is not None:
        in_specs.append(pl.BlockSpec((1, m), lambda i: (0, 0)))
        args.append(post.reshape(1, m))
    kern = functools.partial(_linear_kernel, n_x=len(xs), has_norm=norm is not None,
                             has_gate=gate is not None, has_res=res is not None,
                             has_post=post is not None)
    return pl.pallas_call(
        kern,
        grid=(n // tm,),
        in_specs=in_specs,
        out_specs=pl.BlockSpec((tm, m), lambda i: (i, 0)),
        out_shape=jax.ShapeDtypeStruct((n, m), F32),
        compiler_params=pltpu.CompilerParams(dimension_semantics=("parallel",),
                                             vmem_limit_bytes=VMEM_LIMIT),
        name=name,
    )(*args)


def _sb_in_kernel(x_ref, g_ref, w_ref, qkv_ref, z_ref, kh_ref, vh_ref):
    W = SB_HEADS * SB_DH
    y = _dot(_rms(x_ref[...], g_ref[...]).astype(BF16), w_ref[...])
    k = y[:, W:2 * W]
    v = y[:, 2 * W:3 * W]
    qkv_ref[:, 0:W] = (y[:, 0:W] * (SB_DH ** -0.5 * LOG2E)).astype(BF16)
    qkv_ref[:, W:2 * W] = k.astype(BF16)
    qkv_ref[:, 2 * W:3 * W] = v.astype(BF16)
    z_ref[...] = y[:, 3 * W:]
    for h in range(SB_HEADS):
        kh_ref[0, h] = k[:, h * SB_DH:(h + 1) * SB_DH]
        vh_ref[0, h] = v[:, h * SB_DH:(h + 1) * SB_DH]


def _sb_in(h1, g, w, B, L, tm):
    n, d = h1.shape
    W = SB_HEADS * SB_DH
    tm = min(tm, L)
    nl = L // tm
    head_spec = pl.BlockSpec((1, SB_HEADS, tm, SB_DH), lambda i: (i // nl, 0, i % nl, 0))
    return pl.pallas_call(
        _sb_in_kernel,
        grid=(n // tm,),
        in_specs=[pl.BlockSpec((tm, d), lambda i: (i, 0)), pl.BlockSpec((1, d), lambda i: (0, 0)),
                  pl.BlockSpec((d, 4 * W), lambda i: (0, 0))],
        out_specs=[pl.BlockSpec((tm, 3 * W), lambda i: (i, 0)), pl.BlockSpec((tm, W), lambda i: (i, 0)),
                   head_spec, head_spec],
        out_shape=[jax.ShapeDtypeStruct((n, 3 * W), BF16), jax.ShapeDtypeStruct((n, W), F32),
                   jax.ShapeDtypeStruct((B, SB_HEADS, L, SB_DH), F32),
                   jax.ShapeDtypeStruct((B, SB_HEADS, L, SB_DH), F32)],
        compiler_params=pltpu.CompilerParams(dimension_semantics=("parallel",),
                                             vmem_limit_bytes=VMEM_LIMIT),
        name="sb_in",
    )(h1, g.reshape(1, d), w)


def _unit_lower_inverse(ps, row, col, C):
    eye = jnp.where(row == col, 1.0, 0.0)
    nb = C // INV_BLOCK
    if nb > 1:
        same = (row // INV_BLOCK) == (col // INV_BLOCK)
        pds = [jnp.where(same, p, 0.0) for p in ps]
    else:
        pds = ps
    xs = [eye + pd for pd in pds]
    pws = [_split(pd) for pd in pds]
    steps = INV_BLOCK.bit_length() - 1
    for _ in range(1, steps):
        pws = [_split(_dot3(pw, pw)) for pw in pws]
        xs = [x + _dot3(pw, _split(x)) for pw, x in zip(pws, xs)]
    if nb == 1:
        return xs
    assert nb <= 4
    xss = [_split(x) for x in xs]
    n1s = [_split(_dot3(xh, _split(p - pd))) for xh, p, pd in zip(xss, ps, pds)]
    n2s = [_split(_dot3(n1, n1)) for n1 in n1s]
    ys = [x + _dot3(n1, xh) for x, n1, xh in zip(xs, n1s, xss)]
    return [y + _dot3(n2, _split(y)) for y, n2 in zip(ys, n2s)]


def _ab_kernel(qk_ref, gv_ref, gz_ref, u_ref, dz_ref, sm_ref, wg_ref, bg_ref, gg_ref, cw_ref, hist_ref,
               alog_ref, dtb_ref, gd_ref, sg0_ref, sd0_ref,
               mix_ref, sg_out_ref, sd_out_ref, conv_out_ref, sg_scr, sd_scr, ubuf, *, C, G):
    c = pl.program_id(1)
    R = G * C
    T8 = SUBLANES
    HG = range(GLA_HEADS)
    HD = range(DN_HEADS)
    GS = range(G)
    rs = [slice(g * C, (g + 1) * C) for g in GS]

    @pl.when(c == 0)
    def _():
        sg_scr[...] = sg0_ref[0]
        sd_scr[...] = sd0_ref[0]
        ubuf[0:T8, :] = hist_ref[0]

    u = u_ref[0]
    ubuf[T8:T8 + R, :] = u
    cw = cw_ref[...]
    conv = u * cw[CONV_W - 1:CONV_W, :]
    for i in range(CONV_W - 1):
        off = T8 - (CONV_W - 1) + i
        conv = conv + ubuf[off:off + R, :] * cw[i:i + 1, :]
    cs = conv * _sigmoid(conv)
    tail = u[R - T8:, :]
    ubuf[0:T8, :] = tail

    @pl.when(c == pl.num_programs(1) - 1)
    def _():
        conv_out_ref[0] = tail

    rr = lax.broadcasted_iota(jnp.int32, (R, R), 0)
    rc = lax.broadcasted_iota(jnp.int32, (R, R), 1)
    same = (rr // C) == (rc // C)
    lblk = jnp.where(jnp.logical_and(same, rr >= rc), 1.0, 0.0).astype(BF16)
    ublk = jnp.where(jnp.logical_and(same, rr <= rc), 1.0, 0.0).astype(BF16)
    row = lax.broadcasted_iota(jnp.int32, (C, C), 0)
    col = lax.broadcasted_iota(jnp.int32, (C, C), 1)
    incl = row >= col
    strict = row > col

    sm = sm_ref[0]
    x = _dot(sm.astype(BF16), wg_ref[...]) + bg_ref[...]
    lgh, lgl = _split(-_softplus(-x) * (1.0 / GLA_TAU))
    b = _dot(lblk, lgh) + _dot(lblk, lgl)
    dh, dl = _split(-jnp.exp(alog_ref[...]) * _softplus(sm + dtb_ref[...]))
    b_all = _dot(lblk, dh) + _dot(lblk, dl)
    b_all_t = _dot_tn(dh, ublk) + _dot_tn(dl, ublk)
    beta_all = _sigmoid(sm)

    qk = qk_ref[0]
    q = qk[:, :_GQ] * (GLA_DK ** -0.5)
    k = qk[:, _GQ:]
    mid = C // 2 - 1
    qe, ke, qd, kd, dlast = [], [], [], [], []
    for g in GS:
        bg_ = b[rs[g], :]
        bmid = bg_[mid:mid + 1, :]
        blast = bg_[C - 1:C, :]
        qe.append((q[rs[g], :] * jnp.exp(bg_ - bmid)).astype(BF16))
        ke.append((k[rs[g], :] * jnp.exp(bmid - bg_)).astype(BF16))
        qd.append((q[rs[g], :] * jnp.exp(bg_)).astype(BF16))
        kd.append((k[rs[g], :] * jnp.exp(blast - bg_)).astype(BF16))
        dlast.append(jnp.exp(blast))
    gks = [slice(h * GLA_DK, (h + 1) * GLA_DK) for h in HG]
    gvs = [slice(h * GLA_DV, (h + 1) * GLA_DV) for h in HG]
    gv = gv_ref[0].astype(BF16)
    att = [[jnp.where(incl, _dot_nt(qe[g][:, gks[h]], ke[g][:, gks[h]]), 0.0).astype(BF16) for h in HG]
           for g in GS]

    dhs = [slice(h * DN_DK, (h + 1) * DN_DK) for h in HD]
    nq = DN_HEADS * DN_DK
    qn = [cs[:, dhs[h]] for h in HD]
    kn = [cs[:, nq + h * DN_DK:nq + (h + 1) * DN_DK] for h in HD]
    vn = [cs[:, 2 * nq + h * DN_DV:2 * nq + (h + 1) * DN_DV] for h in HD]
    qn = [t * lax.rsqrt(jnp.sum(t * t, axis=-1, keepdims=True) + EPS) * (DN_DK ** -0.5) for t in qn]
    kn = [t * lax.rsqrt(jnp.sum(t * t, axis=-1, keepdims=True) + EPS) for t in kn]
    qb = [t.astype(BF16) for t in qn]
    kb = [t.astype(BF16) for t in kn]
    GH = [(g, h) for g in GS for h in HD]
    bcol = {(g, h): b_all[rs[g], SM_DEC + h:SM_DEC + h + 1] for g, h in GH}
    brow = {(g, h): b_all_t[SM_DEC + h:SM_DEC + h + 1, rs[g]] for g, h in GH}
    beta = {(g, h): beta_all[rs[g], SM_BETA + h:SM_BETA + h + 1] for g, h in GH}
    dec = {gh: jnp.where(incl, jnp.exp(jnp.where(incl, bcol[gh] - brow[gh], 0.0)), 0.0) for gh in GH}
    kk = {(g, h): _dot_nt(kb[h][rs[g], :], kb[h][rs[g], :]) for g, h in GH}
    qkd = {(g, h): (_dot_nt(qb[h][rs[g], :], kb[h][rs[g], :]) * dec[(g, h)]).astype(BF16) for g, h in GH}
    ps = [jnp.where(strict, -(beta[gh] * dec[gh] * kk[gh]), 0.0) for gh in GH]
    t_inv = dict(zip(GH, [_split(t) for t in _unit_lower_inverse(ps, row, col, C)]))
    eb = {gh: jnp.exp(bcol[gh]) for gh in GH}
    blast_d = {gh: bcol[gh][C - 1:C, :] for gh in GH}
    kdd = {(g, h): (kn[h][rs[g], :] * jnp.exp(blast_d[(g, h)] - bcol[(g, h)])).astype(BF16) for g, h in GH}

    gz = gz_ref[0]
    dz = dz_ref[0]
    gg = gg_ref[...]
    gd = gd_ref[...]
    for g in GS:
        r = rs[g]
        sd = [sd_scr[h] for h in HD]
        sdb = [s.astype(BF16) for s in sd]
        st = [sg_scr[h] for h in HG]
        ksd = [_dot(kb[h][r, :], sdb[h]) for h in HD]
        qsd = [_dot(qb[h][r, :], sdb[h]) for h in HD]
        gvh = [gv[r, gvs[h]] for h in HG]
        oi = [_dot_nt(qd[g][:, gks[h]], st[h].astype(BF16)) for h in HG]
        sg_new = [st[h] * dlast[g][:, gks[h]] + _dot_tn(gvh[h], kd[g][:, gks[h]]) for h in HG]
        og = [_dot(att[g][h], gvh[h]) + oi[h] for h in HG]
        rhs = [beta[(g, h)] * (vn[h][r, :] - eb[(g, h)] * ksd[h]) for h in HD]
        ub = [_dot3(t_inv[(g, h)], _split(rhs[h])).astype(BF16) for h in HD]
        od = [eb[(g, h)] * qsd[h] + _dot(qkd[(g, h)], ub[h]) for h in HD]
        sd_new = [jnp.exp(blast_d[(g, h)]) * sd[h] + _dot_tn(kdd[(g, h)], ub[h]) for h in HD]
        for h in HG:
            sg_scr[h] = sg_new[h]
            zh = gz[r, gvs[h]]
            mix_ref[0, r, gvs[h]] = (_rms(og[h], gg) * (zh * _sigmoid(zh))).astype(BF16)
        for h in HD:
            sd_scr[h] = sd_new[h]
            zh = dz[r, dhs[h]]
            mix_ref[0, r, _GV + h * DN_DV:_GV + (h + 1) * DN_DV] = (
                _rms(od[h], gd) * (zh * _sigmoid(zh))).astype(BF16)

    @pl.when(c == pl.num_programs(1) - 1)
    def _():
        sg_out_ref[0] = sg_scr[...]
        sd_out_ref[0] = sd_scr[...]


def _ab_mix(p3, wg, bg, gg, cw, hist8, alog, dtb, gd, sg0t, sd0, C, G):
    B, L, _ = p3.shape
    R = G * C
    assert L % R == 0
    blk = lambda w, cb: pl.BlockSpec((1, R, w), lambda b, c, cb=cb: (b, c, cb))
    full = lambda shp: pl.BlockSpec(shp, lambda b, c: (0,) * len(shp))
    sg_spec = pl.BlockSpec((1, GLA_HEADS, GLA_DV, GLA_DK), lambda b, c: (b, 0, 0, 0))
    sd_spec = pl.BlockSpec((1, DN_HEADS, DN_DK, DN_DV), lambda b, c: (b, 0, 0, 0))
    hist_spec = pl.BlockSpec((1, SUBLANES, _DC), lambda b, c: (b, 0, 0))
    return pl.pallas_call(
        functools.partial(_ab_kernel, C=C, G=G),
        grid=(B, L // R),
        in_specs=[blk(2 * _GQ, COL_QK // (2 * _GQ)), blk(_GV, COL_GV // _GV), blk(_GV, COL_GZ // _GV),
                  blk(_DC, COL_DC // _DC), blk(_DW, COL_DZ // _DW), blk(LANES, COL_SM // LANES),
                  full((LANES, _GQ)), full((1, _GQ)), full((1, GLA_DV)), full((CONV_W, _DC)), hist_spec,
                  full((1, LANES)), full((1, LANES)), full((1, DN_DV)), sg_spec, sd_spec],
        out_specs=[pl.BlockSpec((1, R, _GV + _DW), lambda b, c: (b, c, 0)), sg_spec, sd_spec, hist_spec],
        out_shape=[jax.ShapeDtypeStruct((B, L, _GV + _DW), BF16),
                   jax.ShapeDtypeStruct((B, GLA_HEADS, GLA_DV, GLA_DK), F32),
                   jax.ShapeDtypeStruct((B, DN_HEADS, DN_DK, DN_DV), F32),
                   jax.ShapeDtypeStruct((B, SUBLANES, _DC), F32)],
        scratch_shapes=[pltpu.VMEM((GLA_HEADS, GLA_DV, GLA_DK), F32),
                        pltpu.VMEM((DN_HEADS, DN_DK, DN_DV), F32),
                        pltpu.VMEM((R + SUBLANES, _DC), F32)],
        compiler_params=pltpu.CompilerParams(dimension_semantics=("parallel", "arbitrary"),
                                             vmem_limit_bytes=VMEM_LIMIT),
        name="ab_mix",
    )(p3, p3, p3, p3, p3, p3, wg, bg, gg, cw, hist8, alog, dtb, gd, sg0t, sd0)


def _sb_kernel(q_ref, k_ref, v_ref, o_ref, acc, carry, zbuf, zbuf2, ebuf, qnorm, kmax, *,
               tq, tk, q_start, slab, nkb):
    i = pl.program_id(2)
    qmin = q_start + i * tq
    jm = qmin // tk
    lane = lax.broadcasted_iota(jnp.int32, (tq, LANES), 1)
    first = lane < SB_DH
    q = q_ref[0]
    zero = jnp.zeros_like(q)
    q2 = jnp.concatenate([jnp.where(first, q, zero), jnp.where(first, zero, q)], axis=0)

    @pl.when(i == 0)
    def _():
        lane_k = lax.broadcasted_iota(jnp.int32, (tk, LANES), 1) < SB_DH

        def kbody(j, m):
            kf = k_ref[0, pl.ds(pl.multiple_of(j * tk, tk), tk), :].astype(F32)
            sq = kf * kf
            n0 = jnp.sqrt(jnp.max(jnp.sum(jnp.where(lane_k, sq, 0.0), axis=1, keepdims=True)))
            n1 = jnp.sqrt(jnp.max(jnp.sum(jnp.where(lane_k, 0.0, sq), axis=1, keepdims=True)))
            m = (jnp.maximum(m[0], n0), jnp.maximum(m[1], n1))
            kmax[0, j] = m[0]
            kmax[1, j] = m[1]
            return m

        lax.fori_loop(0, nkb, kbody, (jnp.float32(0.0), jnp.float32(0.0)))

    qf = q2.astype(F32)
    qnorm[...] = jnp.sqrt(jnp.sum(qf * qf, axis=1, keepdims=True)) * SB_NORM_SLACK
    kr = lax.broadcasted_iota(jnp.int32, (tk, tk), 0)
    kc = lax.broadcasted_iota(jnp.int32, (tk, tk), 1)
    utri = jnp.where(kr >= kc, 1.0, 0.0).astype(BF16)
    n = 2 * tq // slab
    rows = [slice(t * slab, (t + 1) * slab) for t in range(n)]

    def keys(ref, p):
        j = jnp.clip(jm - p, 0, jm)
        return ref[0, pl.ds(pl.multiple_of(j * tk, tk), tk), :]

    acc[...] = jnp.zeros_like(acc)
    carry[...] = jnp.zeros_like(carry)
    ebuf[...] = jnp.full((2 * tq, tk), SB_NEG, F32)
    kb0 = keys(k_ref, 0)
    for t in range(n):
        qpos = qmin + (t * slab + lax.broadcasted_iota(jnp.int32, (slab, tk), 0)) % tq
        kpos = jm * tk + lax.broadcasted_iota(jnp.int32, (slab, tk), 1)
        z0 = _dot_nt(q2[rows[t]], kb0) + jnp.where(kpos < qpos, 0.0, SB_NEG)
        zbuf[rows[t], :] = z0
        zbuf2[rows[t], :] = z0

    def body(state):
        it, stop = state
        jr = jnp.clip(jm - it, 0, nkb - 1)
        m0 = jnp.max(qnorm[0:tq, :] * kmax[0, jr] - carry[0:tq, :])
        m1 = jnp.max(qnorm[tq:2 * tq, :] * kmax[1, jr] - carry[tq:2 * tq, :])
        stop = jnp.where(jnp.maximum(m0, m1) < SB_DEAD, jnp.minimum(stop, it + 1), stop)
        kb = keys(k_ref, it + 1)
        vb = keys(v_ref, it - 1)
        for r in rows:
            w = jnp.exp2(ebuf[r, :])
            acc[r, :] += _dot(w.astype(BF16), vb)
            zz = zbuf[r, :]
            neg_abs = lax.bitcast_convert_type(
                lax.bitcast_convert_type(zz, jnp.int32) | jnp.int32(-2 ** 31), F32)
            sp = jnp.maximum(zz, 0.0) + jnp.log2(1.0 + jnp.exp2(neg_abs))
            cs = _dot(sp.astype(BF16), utri)
            z_next = _dot_nt(q2[r], kb)
            ebuf[r, :] = zbuf2[r, :] - cs - carry[r, :]
            carry[r, :] += cs[:, 0:1]
            zbuf[r, :] = z_next
            zbuf2[r, :] = z_next
        return it + 1, stop

    lax.while_loop(lambda state: state[0] < state[1], body, (jnp.int32(0), jm + 2))
    o_ref[0] = jnp.where(first, acc[0:tq, :], acc[tq:2 * tq, :])


def _sb_attention(q_arr, q_cb, k_arr, k_cb, v_arr, v_cb, *, tq, tk, q_start, name):
    B, Tq = q_arr.shape[:2]
    Tk = k_arr.shape[1]
    assert Tq % tq == 0 and Tk % tk == 0 and tk % tq == 0 and q_start % tk == 0
    assert q_start + Tq <= Tk
    hp = SB_HEADS // 2
    kv_spec = lambda cb0: pl.BlockSpec((1, Tk, LANES), lambda b, h, i, cb0=cb0: (b, 0, cb0 + h))
    slab = min(SB_SLAB, 2 * tq)
    nkb = Tk // tk
    return pl.pallas_call(
        functools.partial(_sb_kernel, tq=tq, tk=tk, q_start=q_start, slab=slab, nkb=nkb),
        grid=(B, hp, Tq // tq),
        in_specs=[pl.BlockSpec((1, tq, LANES), lambda b, h, i: (b, i, q_cb + h)),
                  kv_spec(k_cb), kv_spec(v_cb)],
        out_specs=pl.BlockSpec((1, tq, LANES), lambda b, h, i: (b, i, h)),
        out_shape=jax.ShapeDtypeStruct((B, Tq, SB_HEADS * SB_DH), F32),
        scratch_shapes=[pltpu.VMEM((2 * tq, LANES), F32), pltpu.VMEM((2 * tq, 1), F32),
                        pltpu.VMEM((2 * tq, tk), F32), pltpu.VMEM((2 * tq, tk), F32),
                        pltpu.VMEM((2 * tq, tk), F32), pltpu.VMEM((2 * tq, 1), F32),
                        pltpu.SMEM((2, nkb), F32)],
        compiler_params=pltpu.CompilerParams(dimension_semantics=("parallel", "parallel", "arbitrary"),
                                             vmem_limit_bytes=VMEM_LIMIT),
        name=name,
    )(q_arr, k_arr, v_arr)


def _from_heads(t):
    b, h, l, d = t.shape
    return t.transpose(0, 2, 1, 3).reshape(b, l, h * d)


def _pad_lanes(vec, offset):
    out = jnp.zeros((1, LANES), F32)
    return lax.dynamic_update_slice(out, vec.reshape(1, -1).astype(F32), (0, offset))


def _group(x, s_gla, s_dn, conv_hist, cache_k, cache_v, C, tq, w):
    B, L, D = x.shape
    n = B * L
    x2 = x.reshape(n, D)
    tm = 512

    p = _linear([(x2, 0)], w["ab_w_in"], norm=w["ab_norm"], tm=256, name="ab_in")
    p3 = p.reshape(B, L, AB_COLS)
    hist8 = jnp.pad(conv_hist, ((0, 0), (SUBLANES - (CONV_W - 1), 0), (0, 0)))
    mix, s_gla_t, s_dn_new, conv8 = _ab_mix(
        p3, w["gla_w_gate"], w["gla_b_gate"], w["gla_out_norm"], w["dn_conv_w"], hist8, w["dn_a_log"],
        w["dn_dt_bias"], w["dn_out_norm"], jnp.swapaxes(s_gla, -1, -2), s_dn, C, min(AB_GROUP, L // C))
    h1 = _linear([(mix.reshape(n, _GV + _DW), 0)], w["ab_w_out"], res=(x2, 0), tm=tm, name="ab_out")

    W = SB_HEADS * SB_DH
    qkv, z, k_heads, v_heads = _sb_in(h1, w["sb_norm"], w["sb_w_in"], B, L, 256)
    qkv3 = qkv.reshape(B, L, 3 * W)
    nb = W // LANES
    if cache_k is None:
        o = _sb_attention(qkv3, 0, qkv3, nb, qkv3, 2 * nb, tq=tq, tk=tq, q_start=0, name="sb_prompt")
    else:
        past = cache_k.shape[2]
        tk = 256
        tk_pad = -(-(past + L) // tk) * tk
        pad = ((0, 0), (0, tk_pad - past - L), (0, 0))
        k_all = jnp.pad(jnp.concatenate([_from_heads(cache_k).astype(BF16), qkv3[:, :, W:2 * W]], axis=1), pad)
        v_all = jnp.pad(jnp.concatenate([_from_heads(cache_v).astype(BF16), qkv3[:, :, 2 * W:]], axis=1), pad)
        o = _sb_attention(qkv3, 0, k_all, 0, v_all, 0, tq=tq, tk=tk, q_start=past, name="sb_sample")
    y = _linear([(o.reshape(n, W), 0)], w["sb_w_out"], gate=(z, 0), res=(h1, 0),
                post=w["final_norm"], tm=tm, name="sb_out")
    return (y.reshape(B, L, D), jnp.swapaxes(s_gla_t, -1, -2), s_dn_new,
            conv8[:, SUBLANES - (CONV_W - 1):], k_heads, v_heads)


def kernel(x_prompt, x_sample, state_gla, state_delta, state_conv, cache_k, cache_v, ab_norm, ab_w_in,
           gla_w_gate, gla_b_gate, gla_out_norm, dn_conv_w, dn_a_log, dn_dt_bias, dn_out_norm, ab_w_out,
           sb_norm, sb_w_in, sb_w_out, final_norm):
    assert ab_w_in.shape[0] == 1 and sb_w_in.shape[0] == 1, "one (A|B) layer followed by one C layer"
    D = x_prompt.shape[-1]
    wi = ab_w_in[0]
    offs, acc = [], 0
    for s in (_GQ, _GQ, _GV, GLA_RANK, _GV, _DC, DN_HEADS, DN_HEADS, _DW):
        offs.append((acc, acc + s))
        acc += s
    gq, gk, gv, gr, gz, dqkv, dbeta, da, dz = [wi[:, a:b] for a, b in offs]
    small = jnp.concatenate([gr, dbeta, da], axis=1)
    w_ab = jnp.concatenate(
        [gq, gk, gv, gz, dqkv, dz, small,
         jnp.zeros((D, AB_COLS - COL_SM - small.shape[1]), wi.dtype)], axis=1).astype(BF16)
    wg = jnp.zeros((LANES, _GQ), F32).at[SM_RANK:SM_RANK + GLA_RANK].set(gla_w_gate[0]).astype(BF16)
    w = {
        "ab_norm": ab_norm[0], "ab_w_in": w_ab, "gla_w_gate": wg,
        "gla_b_gate": gla_b_gate[0].reshape(1, _GQ), "gla_out_norm": gla_out_norm[0].reshape(1, GLA_DV),
        "dn_conv_w": dn_conv_w[0], "dn_a_log": _pad_lanes(dn_a_log[0], SM_DEC),
        "dn_dt_bias": _pad_lanes(dn_dt_bias[0], SM_DEC), "dn_out_norm": dn_out_norm[0].reshape(1, DN_DV),
        "ab_w_out": ab_w_out[0].astype(BF16), "sb_norm": sb_norm[0], "sb_w_in": sb_w_in[0].astype(BF16),
        "sb_w_out": sb_w_out[0].astype(BF16), "final_norm": final_norm,
    }
    bp = x_prompt.shape[0]
    yp, gp, dp, cp, kp, vp = _group(
        x_prompt, jnp.zeros((bp, GLA_HEADS, GLA_DK, GLA_DV), F32), jnp.zeros((bp, DN_HEADS, DN_DK, DN_DV), F32),
        jnp.zeros((bp, CONV_W - 1, _DC), F32), None, None, CHUNK, 256, w)
    ys, gs, ds, cs, ks, vs = _group(
        x_sample, state_gla[0], state_delta[0], state_conv[0], cache_k[0], cache_v[0],
        x_sample.shape[1], x_sample.shape[1], w)
    st = lambda a: a[None]
    return (yp, ys, st(gp), st(gs), st(dp), st(ds), st(cp), st(cs), st(kp), st(ks), st(vp), st(vs))
```

```python
import functools

import jax
import jax.numpy as jnp
from jax import lax
from jax.experimental import pallas as pl
from jax.experimental.pallas import tpu as pltpu

F32 = jnp.float32
BF16 = jnp.bfloat16

EPS = 1e-6
CHUNK = 64
GLA_HEADS = 4
GLA_DK = 64
GLA_DV = 128
GLA_RANK = 16
GLA_TAU = 16.0
DN_HEADS = 4
DN_DK = 128
DN_DV = 128
CONV_W = 4
SB_HEADS = 16
SB_DH = 64

LANES = 128
SUBLANES = 8
INV_BLOCK = 16
AB_GROUP = 4
SB_SLAB = 128
SB_NEG = -1e30
SB_DEAD = -160.0
SB_KGROUP = 8
SB_NORM_SLACK = 1.001
LOG2E = 1.4426950408889634
VMEM_LIMIT = 56 * 1024 * 1024

_GQ = GLA_HEADS * GLA_DK
_GV = GLA_HEADS * GLA_DV
_DC = 2 * DN_HEADS * DN_DK + DN_HEADS * DN_DV
_DW = DN_HEADS * DN_DV
COL_QK = 0
COL_GV = 2 * _GQ
COL_GZ = COL_GV + _GV
COL_DC = COL_GZ + _GV
COL_DZ = COL_DC + _DC
COL_SM = COL_DZ + _DW
AB_COLS = 3840
SM_RANK = 0
SM_BETA = GLA_RANK
SM_DEC = GLA_RANK + DN_HEADS


def _sigmoid(x):
    return 1.0 / (1.0 + jnp.exp(-x))


def _softplus(x):
    return jnp.maximum(x, 0.0) + jnp.log(1.0 + jnp.exp(-jnp.abs(x)))


def _split(a):
    hi = a.astype(BF16)
    lo = (a - hi.astype(F32)).astype(BF16)
    return hi, lo


def _dot(a, b):
    return jnp.dot(a, b, preferred_element_type=F32)


def _dot_nt(a, b):
    return lax.dot_general(a, b, (((1,), (1,)), ((), ())), preferred_element_type=F32)


def _dot_tn(a, b):
    return lax.dot_general(a, b, (((0,), (0,)), ((), ())), preferred_element_type=F32)


def _dot3(a, b):
    ah, al = a
    bh, bl = b
    return _dot(ah, bh) + _dot(ah, bl) + _dot(al, bh)


def _rms(x, g):
    return x * lax.rsqrt(jnp.mean(x * x, axis=-1, keepdims=True) + EPS) * g


def _linear_kernel(*refs, n_x, has_norm, has_gate, has_res, has_post):
    refs = list(refs)
    xs = [refs.pop(0) for _ in range(n_x)]
    g_ref = refs.pop(0) if has_norm else None
    z_ref = refs.pop(0) if has_gate else None
    w_ref = refs.pop(0)
    r_ref = refs.pop(0) if has_res else None
    p_ref = refs.pop(0) if has_post else None
    o_ref = refs.pop(0)
    x = xs[0][...] if n_x == 1 else jnp.concatenate([r[...] for r in xs], axis=-1)
    if has_norm:
        x = _rms(x, g_ref[...])
    if has_gate:
        z = z_ref[...]
        x = x * (z * _sigmoid(z))
    y = _dot(x.astype(BF16), w_ref[...])
    if has_res:
        y = y + r_ref[...]
    if has_post:
        y = _rms(y, p_ref[...])
    o_ref[...] = y


def _linear(xs, w, *, norm=None, gate=None, res=None, post=None, tm, name):
    n = xs[0][0].shape[0]
    k, m = w.shape
    tm = min(tm, n)
    assert n % tm == 0
    in_specs, args = [], []
    kx = k // len(xs)
    for arr, cb in xs:
        in_specs.append(pl.BlockSpec((tm, kx), lambda i, cb=cb: (i, cb)))
        args.append(arr)
    if norm is not None:
        in_specs.append(pl.BlockSpec((1, k), lambda i: (0, 0)))
        args.append(norm.reshape(1, k))
    if gate is not None:
        in_specs.append(pl.BlockSpec((tm, k), lambda i, cb=gate[1]: (i, cb)))
        args.append(gate[0])
    in_specs.append(pl.BlockSpec((k, m), lambda i: (0, 0)))
    args.append(w)
    if res is not None:
        in_specs.append(pl.BlockSpec((tm, m), lambda i, cb=res[1]: (i, cb)))
        args.append(res[0])
    if post is not None:
        in_specs.append(pl.BlockSpec((1, m), lambda i: (0, 0)))
        args.append(post.reshape(1, m))
    kern = functools.partial(_linear_kernel, n_x=len(xs), has_norm=norm is not None,
                             has_gate=gate is not None, has_res=res is not None,
                             has_post=post is not None)
    return pl.pallas_call(
        kern,
        grid=(n // tm,),
        in_specs=in_specs,
        out_specs=pl.BlockSpec((tm, m), lambda i: (i, 0)),
        out_shape=jax.ShapeDtypeStruct((n, m), F32),
        compiler_params=pltpu.CompilerParams(dimension_semantics=("parallel",),
                                             vmem_limit_bytes=VMEM_LIMIT),
        name=name,
    )(*args)


def _sb_in_kernel(x_ref, g_ref, w_ref, qkv_ref, z_ref, kh_ref, vh_ref):
    W = SB_HEADS * SB_DH
    y = _dot(_rms(x_ref[...], g_ref[...]).astype(BF16), w_ref[...])
    k = y[:, W:2 * W]
    v = y[:, 2 * W:3 * W]
    qkv_ref[:, 0:W] = (y[:, 0:W] * (SB_DH ** -0.5 * LOG2E)).astype(BF16)
    qkv_ref[:, W:2 * W] = k.astype(BF16)
    qkv_ref[:, 2 * W:3 * W] = v.astype(BF16)
    z_ref[...] = y[:, 3 * W:]
    for h in range(SB_HEADS):
        kh_ref[0, h] = k[:, h * SB_DH:(h + 1) * SB_DH]
        vh_ref[0, h] = v[:, h * SB_DH:(h + 1) * SB_DH]


def _sb_in(h1, g, w, B, L, tm):
    n, d = h1.shape
    W = SB_HEADS * SB_DH
    tm = min(tm, L)
    nl = L // tm
    head_spec = pl.BlockSpec((1, SB_HEADS, tm, SB_DH), lambda i: (i // nl, 0, i % nl, 0))
    return pl.pallas_call(
        _sb_in_kernel,
        grid=(n // tm,),
        in_specs=[pl.BlockSpec((tm, d), lambda i: (i, 0)), pl.BlockSpec((1, d), lambda i: (0, 0)),
                  pl.BlockSpec((d, 4 * W), lambda i: (0, 0))],
        out_specs=[pl.BlockSpec((tm, 3 * W), lambda i: (i, 0)), pl.BlockSpec((tm, W), lambda i: (i, 0)),
                   head_spec, head_spec],
        out_shape=[jax.ShapeDtypeStruct((n, 3 * W), BF16), jax.ShapeDtypeStruct((n, W), F32),
                   jax.ShapeDtypeStruct((B, SB_HEADS, L, SB_DH), F32),
                   jax.ShapeDtypeStruct((B, SB_HEADS, L, SB_DH), F32)],
        compiler_params=pltpu.CompilerParams(dimension_semantics=("parallel",),
                                             vmem_limit_bytes=VMEM_LIMIT),
        name="sb_in",
    )(h1, g.reshape(1, d), w)


def _unit_lower_inverse(ps, row, col, C):
    eye = jnp.where(row == col, 1.0, 0.0)
    nb = C // INV_BLOCK
    if nb > 1:
        same = (row // INV_BLOCK) == (col // INV_BLOCK)
        pds = [jnp.where(same, p, 0.0) for p in ps]
    else:
        pds = ps
    xs = [eye + pd for pd in pds]
    pws = [_split(pd) for pd in pds]
    steps = INV_BLOCK.bit_length() - 1
    for _ in range(1, steps):
        pws = [_split(_dot3(pw, pw)) for pw in pws]
        xs = [x + _dot3(pw, _split(x)) for pw, x in zip(pws, xs)]
    if nb == 1:
        return xs
    assert nb <= 4
    xss = [_split(x) for x in xs]
    n1s = [_split(_dot3(xh, _split(p - pd))) for xh, p, pd in zip(xss, ps, pds)]
    n2s = [_split(_dot3(n1, n1)) for n1 in n1s]
    ys = [x + _dot3(n1, xh) for x, n1, xh in zip(xs, n1s, xss)]
    return [y + _dot3(n2, _split(y)) for y, n2 in zip(ys, n2s)]


def _ab_kernel(qk_ref, gv_ref, gz_ref, u_ref, dz_ref, sm_ref, wg_ref, bg_ref, gg_ref, cw_ref, hist_ref,
               alog_ref, dtb_ref, gd_ref, sg0_ref, sd0_ref,
               mix_ref, sg_out_ref, sd_out_ref, conv_out_ref, sg_scr, sd_scr, ubuf, *, C, G):
    c = pl.program_id(1)
    R = G * C
    T8 = SUBLANES
    HG = range(GLA_HEADS)
    HD = range(DN_HEADS)
    GS = range(G)
    rs = [slice(g * C, (g + 1) * C) for g in GS]

    @pl.when(c == 0)
    def _():
        sg_scr[...] = sg0_ref[0]
        sd_scr[...] = sd0_ref[0]
        ubuf[0:T8, :] = hist_ref[0]

    u = u_ref[0]
    ubuf[T8:T8 + R, :] = u
    cw = cw_ref[...]
    conv = u * cw[CONV_W - 1:CONV_W, :]
    for i in range(CONV_W - 1):
        off = T8 - (CONV_W - 1) + i
        conv = conv + ubuf[off:off + R, :] * cw[i:i + 1, :]
    cs = conv * _sigmoid(conv)
    tail = u[R - T8:, :]
    ubuf[0:T8, :] = tail

    @pl.when(c == pl.num_programs(1) - 1)
    def _():
        conv_out_ref[0] = tail

    rr = lax.broadcasted_iota(jnp.int32, (R, R), 0)
    rc = lax.broadcasted_iota(jnp.int32, (R, R), 1)
    same = (rr // C) == (rc // C)
    lblk = jnp.where(jnp.logical_and(same, rr >= rc), 1.0, 0.0).astype(BF16)
    ublk = jnp.where(jnp.logical_and(same, rr <= rc), 1.0, 0.0).astype(BF16)
    row = lax.broadcasted_iota(jnp.int32, (C, C), 0)
    col = lax.broadcasted_iota(jnp.int32, (C, C), 1)
    incl = row >= col
    strict = row > col

    sm = sm_ref[0]
    x = _dot(sm.astype(BF16), wg_ref[...]) + bg_ref[...]
    lgh, lgl = _split(-_softplus(-x) * (1.0 / GLA_TAU))
    b = _dot(lblk, lgh) + _dot(lblk, lgl)
    dh, dl = _split(-jnp.exp(alog_ref[...]) * _softplus(sm + dtb_ref[...]))
    b_all = _dot(lblk, dh) + _dot(lblk, dl)
    b_all_t = _dot_tn(dh, ublk) + _dot_tn(dl, ublk)
    beta_all = _sigmoid(sm)

    qk = qk_ref[0]
    q = qk[:, :_GQ] * (GLA_DK ** -0.5)
    k = qk[:, _GQ:]
    mid = C // 2 - 1
    qe, ke, qd, kd, dlast = [], [], [], [], []
    for g in GS:
        bg_ = b[rs[g], :]
        bmid = bg_[mid:mid + 1, :]
        blast = bg_[C - 1:C, :]
        qe.append((q[rs[g], :] * jnp.exp(bg_ - bmid)).astype(BF16))
        ke.append((k[rs[g], :] * jnp.exp(bmid - bg_)).astype(BF16))
        qd.append((q[rs[g], :] * jnp.exp(bg_)).astype(BF16))
        kd.append((k[rs[g], :] * jnp.exp(blast - bg_)).astype(BF16))
        dlast.append(jnp.exp(blast))
    gks = [slice(h * GLA_DK, (h + 1) * GLA_DK) for h in HG]
    gvs = [slice(h * GLA_DV, (h + 1) * GLA_DV) for h in HG]
    gv = gv_ref[0].astype(BF16)
    att = [[jnp.where(incl, _dot_nt(qe[g][:, gks[h]], ke[g][:, gks[h]]), 0.0).astype(BF16) for h in HG]
           for g in GS]

    dhs = [slice(h * DN_DK, (h + 1) * DN_DK) for h in HD]
    nq = DN_HEADS * DN_DK
    qn = [cs[:, dhs[h]] for h in HD]
    kn = [cs[:, nq + h * DN_DK:nq + (h + 1) * DN_DK] for h in HD]
    vn = [cs[:, 2 * nq + h * DN_DV:2 * nq + (h + 1) * DN_DV] for h in HD]
    qn = [t * lax.rsqrt(jnp.sum(t * t, axis=-1, keepdims=True) + EPS) * (DN_DK ** -0.5) for t in qn]
    kn = [t * lax.rsqrt(jnp.sum(t * t, axis=-1, keepdims=True) + EPS) for t in kn]
    qb = [t.astype(BF16) for t in qn]
    kb = [t.astype(BF16) for t in kn]
    GH = [(g, h) for g in GS for h in HD]
    bcol = {(g, h): b_all[rs[g], SM_DEC + h:SM_DEC + h + 1] for g, h in GH}
    brow = {(g, h): b_all_t[SM_DEC + h:SM_DEC + h + 1, rs[g]] for g, h in GH}
    beta = {(g, h): beta_all[rs[g], SM_BETA + h:SM_BETA + h + 1] for g, h in GH}
    dec = {gh: jnp.where(incl, jnp.exp(jnp.where(incl, bcol[gh] - brow[gh], 0.0)), 0.0) for gh in GH}
    kk = {(g, h): _dot_nt(kb[h][rs[g], :], kb[h][rs[g], :]) for g, h in GH}
    qkd = {(g, h): (_dot_nt(qb[h][rs[g], :], kb[h][rs[g], :]) * dec[(g, h)]).astype(BF16) for g, h in GH}
    ps = [jnp.where(strict, -(beta[gh] * dec[gh] * kk[gh]), 0.0) for gh in GH]
    t_inv = dict(zip(GH, [_split(t) for t in _unit_lower_inverse(ps, row, col, C)]))
    eb = {gh: jnp.exp(bcol[gh]) for gh in GH}
    blast_d = {gh: bcol[gh][C - 1:C, :] for gh in GH}
    kdd = {(g, h): (kn[h][rs[g], :] * jnp.exp(blast_d[(g, h)] - bcol[(g, h)])).astype(BF16) for g, h in GH}

    gz = gz_ref[0]
    dz = dz_ref[0]
    gg = gg_ref[...]
    gd = gd_ref[...]
    for g in GS:
        r = rs[g]
        sd = [sd_scr[h] for h in HD]
        sdb = [s.astype(BF16) for s in sd]
        st = [sg_scr[h] for h in HG]
        ksd = [_dot(kb[h][r, :], sdb[h]) for h in HD]
        qsd = [_dot(qb[h][r, :], sdb[h]) for h in HD]
        gvh = [gv[r, gvs[h]] for h in HG]
        oi = [_dot_nt(qd[g][:, gks[h]], st[h].astype(BF16)) for h in HG]
        sg_new = [st[h] * dlast[g][:, gks[h]] + _dot_tn(gvh[h], kd[g][:, gks[h]]) for h in HG]
        og = [_dot(att[g][h], gvh[h]) + oi[h] for h in HG]
        rhs = [beta[(g, h)] * (vn[h][r, :] - eb[(g, h)] * ksd[h]) for h in HD]
        ub = [_dot3(t_inv[(g, h)], _split(rhs[h])).astype(BF16) for h in HD]
        od = [eb[(g, h)] * qsd[h] + _dot(qkd[(g, h)], ub[h]) for h in HD]
        sd_new = [jnp.exp(blast_d[(g, h)]) * sd[h] + _dot_tn(kdd[(g, h)], ub[h]) for h in HD]
        for h in HG:
            sg_scr[h] = sg_new[h]
            zh = gz[r, gvs[h]]
            mix_ref[0, r, gvs[h]] = (_rms(og[h], gg) * (zh * _sigmoid(zh))).astype(BF16)
        for h in HD:
            sd_scr[h] = sd_new[h]
            zh = dz[r, dhs[h]]
            mix_ref[0, r, _GV + h * DN_DV:_GV + (h + 1) * DN_DV] = (
                _rms(od[h], gd) * (zh * _sigmoid(zh))).astype(BF16)

    @pl.when(c == pl.num_programs(1) - 1)
    def _():
        sg_out_ref[0] = sg_scr[...]
        sd_out_ref[0] = sd_scr[...]


def _ab_mix(p3, wg, bg, gg, cw, hist8, alog, dtb, gd, sg0t, sd0, C, G):
    B, L, _ = p3.shape
    R = G * C
    assert L % R == 0
    blk = lambda w, cb: pl.BlockSpec((1, R, w), lambda b, c, cb=cb: (b, c, cb))
    full = lambda shp: pl.BlockSpec(shp, lambda b, c: (0,) * len(shp))
    sg_spec = pl.BlockSpec((1, GLA_HEADS, GLA_DV, GLA_DK), lambda b, c: (b, 0, 0, 0))
    sd_spec = pl.BlockSpec((1, DN_HEADS, DN_DK, DN_DV), lambda b, c: (b, 0, 0, 0))
    hist_spec = pl.BlockSpec((1, SUBLANES, _DC), lambda b, c: (b, 0, 0))
    return pl.pallas_call(
        functools.partial(_ab_kernel, C=C, G=G),
        grid=(B, L // R),
        in_specs=[blk(2 * _GQ, COL_QK // (2 * _GQ)), blk(_GV, COL_GV // _GV), blk(_GV, COL_GZ // _GV),
                  blk(_DC, COL_DC // _DC), blk(_DW, COL_DZ // _DW), blk(LANES, COL_SM // LANES),
                  full((LANES, _GQ)), full((1, _GQ)), full((1, GLA_DV)), full((CONV_W, _DC)), hist_spec,
                  full((1, LANES)), full((1, LANES)), full((1, DN_DV)), sg_spec, sd_spec],
        out_specs=[pl.BlockSpec((1, R, _GV + _DW), lambda b, c: (b, c, 0)), sg_spec, sd_spec, hist_spec],
        out_shape=[jax.ShapeDtypeStruct((B, L, _GV + _DW), BF16),
                   jax.ShapeDtypeStruct((B, GLA_HEADS, GLA_DV, GLA_DK), F32),
                   jax.ShapeDtypeStruct((B, DN_HEADS, DN_DK, DN_DV), F32),
                   jax.ShapeDtypeStruct((B, SUBLANES, _DC), F32)],
        scratch_shapes=[pltpu.VMEM((GLA_HEADS, GLA_DV, GLA_DK), F32),
                        pltpu.VMEM((DN_HEADS, DN_DK, DN_DV), F32),
                        pltpu.VMEM((R + SUBLANES, _DC), F32)],
        compiler_params=pltpu.CompilerParams(dimension_semantics=("parallel", "arbitrary"),
                                             vmem_limit_bytes=VMEM_LIMIT),
        name="ab_mix",
    )(p3, p3, p3, p3, p3, p3, wg, bg, gg, cw, hist8, alog, dtb, gd, sg0t, sd0)


def _sb_kernel(q_ref, k_ref, v_ref, o_ref, acc, carry, zbuf, zbuf2, ebuf, qnorm, kmax, *,
               tq, tk, q_start, slab, nkb):
    i = pl.program_id(2)
    qmin = q_start + i * tq
    jm = qmin // tk
    lane = lax.broadcasted_iota(jnp.int32, (tq, LANES), 1)
    first = lane < SB_DH
    q = q_ref[0]
    zero = jnp.zeros_like(q)
    q2 = jnp.concatenate([jnp.where(first, q, zero), jnp.where(first, zero, q)], axis=0)

    kg = SB_KGROUP if nkb % SB_KGROUP == 0 else 1

    @pl.when(i == 0)
    def _():
        lane_k = lax.broadcasted_iota(jnp.int32, (kg * tk, LANES), 1) < SB_DH

        def kbody(jg, m):
            kf = k_ref[0, pl.ds(pl.multiple_of(jg * (kg * tk), kg * tk), kg * tk), :].astype(F32)
            sq = kf * kf
            n0 = jnp.sqrt(jnp.max(jnp.sum(jnp.where(lane_k, sq, 0.0), axis=1, keepdims=True)))
            n1 = jnp.sqrt(jnp.max(jnp.sum(jnp.where(lane_k, 0.0, sq), axis=1, keepdims=True)))
            m = (jnp.maximum(m[0], n0), jnp.maximum(m[1], n1))
            for t in range(kg):
                kmax[0, jg * kg + t] = m[0]
                kmax[1, jg * kg + t] = m[1]
            return m

        lax.fori_loop(0, nkb // kg, kbody, (jnp.float32(0.0), jnp.float32(0.0)))

    qf = q2.astype(F32)
    qnorm[...] = jnp.sqrt(jnp.sum(qf * qf, axis=1, keepdims=True)) * SB_NORM_SLACK
    kr = lax.broadcasted_iota(jnp.int32, (tk, tk), 0)
    kc = lax.broadcasted_iota(jnp.int32, (tk, tk), 1)
    utri = jnp.where(kr >= kc, 1.0, 0.0).astype(BF16)
    n = 2 * tq // slab
    rows = [slice(t * slab, (t + 1) * slab) for t in range(n)]

    def keys(ref, p):
        j = jnp.clip(jm - p, 0, jm)
        return ref[0, pl.ds(pl.multiple_of(j * tk, tk), tk), :]

    acc[...] = jnp.zeros_like(acc)
    kb0 = keys(k_ref, 0)
    kb1 = keys(k_ref, 1)
    for t in range(n):
        r = rows[t]
        qpos = qmin + (t * slab + lax.broadcasted_iota(jnp.int32, (slab, tk), 0)) % tq
        kpos = jm * tk + lax.broadcasted_iota(jnp.int32, (slab, tk), 1)
        z0 = _dot_nt(q2[r], kb0) + jnp.where(kpos < qpos, 0.0, SB_NEG)
        z1 = _dot_nt(q2[r], kb1)
        neg_abs = lax.bitcast_convert_type(
            lax.bitcast_convert_type(z0, jnp.int32) | jnp.int32(-2 ** 31), F32)
        sp = jnp.maximum(z0, 0.0) + jnp.log2(1.0 + jnp.exp2(neg_abs))
        cs = _dot(sp.astype(BF16), utri)
        ebuf[r, :] = z0 - cs
        carry[r, :] = cs[:, 0:1]
        zbuf[r, :] = z1
        zbuf2[r, :] = z1

    def body(state):
        it, stop = state
        jr = jnp.clip(jm - it, 0, nkb - 1)
        m0 = jnp.max(qnorm[0:tq, :] * kmax[0, jr] - carry[0:tq, :])
        m1 = jnp.max(qnorm[tq:2 * tq, :] * kmax[1, jr] - carry[tq:2 * tq, :])
        stop = jnp.where(jnp.maximum(m0, m1) < SB_DEAD, jnp.minimum(stop, it + 1), stop)
        kb = keys(k_ref, it + 1)
        vb = keys(v_ref, it - 1)
        for r in rows:
            w = jnp.exp2(ebuf[r, :])
            acc[r, :] += _dot(w.astype(BF16), vb)
            zz = zbuf[r, :]
            neg_abs = lax.bitcast_convert_type(
                lax.bitcast_convert_type(zz, jnp.int32) | jnp.int32(-2 ** 31), F32)
            sp = jnp.maximum(zz, 0.0) + jnp.log2(1.0 + jnp.exp2(neg_abs))
            cs = _dot(sp.astype(BF16), utri)
            z_next = _dot_nt(q2[r], kb)
            ebuf[r, :] = zbuf2[r, :] - cs - carry[r, :]
            carry[r, :] += cs[:, 0:1]
            zbuf[r, :] = z_next
            zbuf2[r, :] = z_next
        return it + 1, stop

    lax.while_loop(lambda state: state[0] < state[1], body, (jnp.int32(1), jm + 2))
    o_ref[0] = jnp.where(first, acc[0:tq, :], acc[tq:2 * tq, :])


def _sb_attention(q_arr, q_cb, k_arr, k_cb, v_arr, v_cb, *, tq, tk, q_start, name):
    B, Tq = q_arr.shape[:2]
    Tk = k_arr.shape[1]
    assert Tq % tq == 0 and Tk % tk == 0 and tk % tq == 0 and q_start % tk == 0
    assert q_start + Tq <= Tk
    hp = SB_HEADS // 2
    kv_spec = lambda cb0: pl.BlockSpec((1, Tk, LANES), lambda b, h, i, cb0=cb0: (b, 0, cb0 + h))
    slab = min(SB_SLAB, 2 * tq)
    nkb = Tk // tk
    return pl.pallas_call(
        functools.partial(_sb_kernel, tq=tq, tk=tk, q_start=q_start, slab=slab, nkb=nkb),
        grid=(B, hp, Tq // tq),
        in_specs=[pl.BlockSpec((1, tq, LANES), lambda b, h, i: (b, i, q_cb + h)),
                  kv_spec(k_cb), kv_spec(v_cb)],
        out_specs=pl.BlockSpec((1, tq, LANES), lambda b, h, i: (b, i, h)),
        out_shape=jax.ShapeDtypeStruct((B, Tq, SB_HEADS * SB_DH), F32),
        scratch_shapes=[pltpu.VMEM((2 * tq, LANES), F32), pltpu.VMEM((2 * tq, 1), F32),
                        pltpu.VMEM((2 * tq, tk), F32), pltpu.VMEM((2 * tq, tk), F32),
                        pltpu.VMEM((2 * tq, tk), F32), pltpu.VMEM((2 * tq, 1), F32),
                        pltpu.SMEM((2, nkb), F32)],
        compiler_params=pltpu.CompilerParams(dimension_semantics=("parallel", "parallel", "arbitrary"),
                                             vmem_limit_bytes=VMEM_LIMIT),
        name=name,
    )(q_arr, k_arr, v_arr)


def _from_heads(t):
    b, h, l, d = t.shape
    return t.transpose(0, 2, 1, 3).reshape(b, l, h * d)


def _pad_lanes(vec, offset):
    out = jnp.zeros((1, LANES), F32)
    return lax.dynamic_update_slice(out, vec.reshape(1, -1).astype(F32), (0, offset))


def _group(x, s_gla, s_dn, conv_hist, cache_k, cache_v, C, tq, w):
    B, L, D = x.shape
    n = B * L
    x2 = x.reshape(n, D)
    tm = 512

    p = _linear([(x2, 0)], w["ab_w_in"], norm=w["ab_norm"], tm=256, name="ab_in")
    p3 = p.reshape(B, L, AB_COLS)
    hist8 = jnp.pad(conv_hist, ((0, 0), (SUBLANES - (CONV_W - 1), 0), (0, 0)))
    mix, s_gla_t, s_dn_new, conv8 = _ab_mix(
        p3, w["gla_w_gate"], w["gla_b_gate"], w["gla_out_norm"], w["dn_conv_w"], hist8, w["dn_a_log"],
        w["dn_dt_bias"], w["dn_out_norm"], jnp.swapaxes(s_gla, -1, -2), s_dn, C, min(AB_GROUP, L // C))
    h1 = _linear([(mix.reshape(n, _GV + _DW), 0)], w["ab_w_out"], res=(x2, 0), tm=tm, name="ab_out")

    W = SB_HEADS * SB_DH
    qkv, z, k_heads, v_heads = _sb_in(h1, w["sb_norm"], w["sb_w_in"], B, L, 256)
    qkv3 = qkv.reshape(B, L, 3 * W)
    nb = W // LANES
    if cache_k is None:
        o = _sb_attention(qkv3, 0, qkv3, nb, qkv3, 2 * nb, tq=tq, tk=tq, q_start=0, name="sb_prompt")
    else:
        past = cache_k.shape[2]
        tk = 256
        tk_pad = -(-(past + L) // tk) * tk
        pad = ((0, 0), (0, tk_pad - past - L), (0, 0))
        k_all = jnp.pad(jnp.concatenate([_from_heads(cache_k).astype(BF16), qkv3[:, :, W:2 * W]], axis=1), pad)
        v_all = jnp.pad(jnp.concatenate([_from_heads(cache_v).astype(BF16), qkv3[:, :, 2 * W:]], axis=1), pad)
        o = _sb_attention(qkv3, 0, k_all, 0, v_all, 0, tq=tq, tk=tk, q_start=past, name="sb_sample")
    y = _linear([(o.reshape(n, W), 0)], w["sb_w_out"], gate=(z, 0), res=(h1, 0),
                post=w["final_norm"], tm=tm, name="sb_out")
    return (y.reshape(B, L, D), jnp.swapaxes(s_gla_t, -1, -2), s_dn_new,
            conv8[:, SUBLANES - (CONV_W - 1):], k_heads, v_heads)


def kernel(x_prompt, x_sample, state_gla, state_delta, state_conv, cache_k, cache_v, ab_norm, ab_w_in,
           gla_w_gate, gla_b_gate, gla_out_norm, dn_conv_w, dn_a_log, dn_dt_bias, dn_out_norm, ab_w_out,
           sb_norm, sb_w_in, sb_w_out, final_norm):
    assert ab_w_in.shape[0] == 1 and sb_w_in.shape[0] == 1, "one (A|B) layer followed by one C layer"
    D = x_prompt.shape[-1]
    wi = ab_w_in[0]
    offs, acc = [], 0
    for s in (_GQ, _GQ, _GV, GLA_RANK, _GV, _DC, DN_HEADS, DN_HEADS, _DW):
        offs.append((acc, acc + s))
        acc += s
    gq, gk, gv, gr, gz, dqkv, dbeta, da, dz = [wi[:, a:b] for a, b in offs]
    small = jnp.concatenate([gr, dbeta, da], axis=1)
    w_ab = jnp.concatenate(
        [gq, gk, gv, gz, dqkv, dz, small,
         jnp.zeros((D, AB_COLS - COL_SM - small.shape[1]), wi.dtype)], axis=1).astype(BF16)
    wg = jnp.zeros((LANES, _GQ), F32).at[SM_RANK:SM_RANK + GLA_RANK].set(gla_w_gate[0]).astype(BF16)
    w = {
        "ab_norm": ab_norm[0], "ab_w_in": w_ab, "gla_w_gate": wg,
        "gla_b_gate": gla_b_gate[0].reshape(1, _GQ), "gla_out_norm": gla_out_norm[0].reshape(1, GLA_DV),
        "dn_conv_w": dn_conv_w[0], "dn_a_log": _pad_lanes(dn_a_log[0], SM_DEC),
        "dn_dt_bias": _pad_lanes(dn_dt_bias[0], SM_DEC), "dn_out_norm": dn_out_norm[0].reshape(1, DN_DV),
        "ab_w_out": ab_w_out[0].astype(BF16), "sb_norm": sb_norm[0], "sb_w_in": sb_w_in[0].astype(BF16),
        "sb_w_out": sb_w_out[0].astype(BF16), "final_norm": final_norm,
    }
    bp = x_prompt.shape[0]
    yp, gp, dp, cp, kp, vp = _group(
        x_prompt, jnp.zeros((bp, GLA_HEADS, GLA_DK, GLA_DV), F32), jnp.zeros((bp, DN_HEADS, DN_DK, DN_DV), F32),
        jnp.zeros((bp, CONV_W - 1, _DC), F32), None, None, CHUNK, 256, w)
    ys, gs, ds, cs, ks, vs = _group(
        x_sample, state_gla[0], state_delta[0], state_conv[0], cache_k[0], cache_v[0],
        x_sample.shape[1], x_sample.shape[1], w)
    st = lambda a: a[None]
    return (yp, ys, st(gp), st(gs), st(dp), st(ds), st(cp), st(cs), st(kp), st(ks), st(vp), st(vs))
```

```python
import functools

import jax
import jax.numpy as jnp
from jax import lax
from jax.experimental import pallas as pl
from jax.experimental.pallas import tpu as pltpu

F32 = jnp.float32
BF16 = jnp.bfloat16

EPS = 1e-6
CHUNK = 64
GLA_HEADS = 4
GLA_DK = 64
GLA_DV = 128
GLA_RANK = 16
GLA_TAU = 16.0
DN_HEADS = 4
DN_DK = 128
DN_DV = 128
CONV_W = 4
SB_HEADS = 16
SB_DH = 64

LANES = 128
SUBLANES = 8
INV_BLOCK = 16
AB_GROUP = 4
SB_SLAB = 128
SB_NEG = -1e30
SB_DEAD = -160.0
SB_KGROUP = 8
SB_NORM_SLACK = 1.001
LOG2E = 1.4426950408889634
VMEM_LIMIT = 56 * 1024 * 1024

_GQ = GLA_HEADS * GLA_DK
_GV = GLA_HEADS * GLA_DV
_DC = 2 * DN_HEADS * DN_DK + DN_HEADS * DN_DV
_DW = DN_HEADS * DN_DV
COL_QK = 0
COL_GV = 2 * _GQ
COL_GZ = COL_GV + _GV
COL_DC = COL_GZ + _GV
COL_DZ = COL_DC + _DC
COL_SM = COL_DZ + _DW
AB_COLS = 3840
SM_RANK = 0
SM_BETA = GLA_RANK
SM_DEC = GLA_RANK + DN_HEADS


def _sigmoid(x):
    return 1.0 / (1.0 + jnp.exp(-x))


def _softplus(x):
    return jnp.maximum(x, 0.0) + jnp.log(1.0 + jnp.exp(-jnp.abs(x)))


def _split(a):
    hi = a.astype(BF16)
    lo = (a - hi.astype(F32)).astype(BF16)
    return hi, lo


def _dot(a, b):
    return jnp.dot(a, b, preferred_element_type=F32)


def _dot_nt(a, b):
    return lax.dot_general(a, b, (((1,), (1,)), ((), ())), preferred_element_type=F32)


def _dot_tn(a, b):
    return lax.dot_general(a, b, (((0,), (0,)), ((), ())), preferred_element_type=F32)


def _dot3(a, b):
    ah, al = a
    bh, bl = b
    return _dot(ah, bh) + _dot(ah, bl) + _dot(al, bh)


def _rms(x, g):
    return x * lax.rsqrt(jnp.mean(x * x, axis=-1, keepdims=True) + EPS) * g


def _linear_kernel(*refs, n_x, has_norm, has_gate, has_res, has_post):
    refs = list(refs)
    xs = [refs.pop(0) for _ in range(n_x)]
    g_ref = refs.pop(0) if has_norm else None
    z_ref = refs.pop(0) if has_gate else None
    w_ref = refs.pop(0)
    r_ref = refs.pop(0) if has_res else None
    p_ref = refs.pop(0) if has_post else None
    o_ref = refs.pop(0)
    x = xs[0][...] if n_x == 1 else jnp.concatenate([r[...] for r in xs], axis=-1)
    if has_norm:
        x = _rms(x, g_ref[...])
    if has_gate:
        z = z_ref[...]
        x = x * (z * _sigmoid(z))
    y = _dot(x.astype(BF16), w_ref[...])
    if has_res:
        y = y + r_ref[...]
    if has_post:
        y = _rms(y, p_ref[...])
    o_ref[...] = y


def _linear(xs, w, *, norm=None, gate=None, res=None, post=None, tm, name):
    n = xs[0][0].shape[0]
    k, m = w.shape
    tm = min(tm, n)
    assert n % tm == 0
    in_specs, args = [], []
    kx = k // len(xs)
    for arr, cb in xs:
        in_specs.append(pl.BlockSpec((tm, kx), lambda i, cb=cb: (i, cb)))
        args.append(arr)
    if norm is not None:
        in_specs.append(pl.BlockSpec((1, k), lambda i: (0, 0)))
        args.append(norm.reshape(1, k))
    if gate is not None:
        in_specs.append(pl.BlockSpec((tm, k), lambda i, cb=gate[1]: (i, cb)))
        args.append(gate[0])
    in_specs.append(pl.BlockSpec((k, m), lambda i: (0, 0)))
    args.append(w)
    if res is not None:
        in_specs.append(pl.BlockSpec((tm, m), lambda i, cb=res[1]: (i, cb)))
        args.append(res[0])
    if post is not None:
        in_specs.append(pl.BlockSpec((1, m), lambda i: (0, 0)))
        args.append(post.reshape(1, m))
    kern = functools.partial(_linear_kernel, n_x=len(xs), has_norm=norm is not None,
                             has_gate=gate is not None, has_res=res is not None,
                             has_post=post is not None)
    return pl.pallas_call(
        kern,
        grid=(n // tm,),
        in_specs=in_specs,
        out_specs=pl.BlockSpec((tm, m), lambda i: (i, 0)),
        out_shape=jax.ShapeDtypeStruct((n, m), F32),
        compiler_params=pltpu.CompilerParams(dimension_semantics=("parallel",),
                                             vmem_limit_bytes=VMEM_LIMIT),
        name=name,
    )(*args)


def _sb_in_kernel(x_ref, g_ref, w_ref, qkv_ref, z_ref, kh_ref, vh_ref):
    W = SB_HEADS * SB_DH
    y = _dot(_rms(x_ref[...], g_ref[...]).astype(BF16), w_ref[...])
    k = y[:, W:2 * W]
    v = y[:, 2 * W:3 * W]
    qkv_ref[:, 0:W] = (y[:, 0:W] * (SB_DH ** -0.5 * LOG2E)).astype(BF16)
    qkv_ref[:, W:2 * W] = k.astype(BF16)
    qkv_ref[:, 2 * W:3 * W] = v.astype(BF16)
    z_ref[...] = y[:, 3 * W:]
    for h in range(SB_HEADS):
        kh_ref[0, h] = k[:, h * SB_DH:(h + 1) * SB_DH]
        vh_ref[0, h] = v[:, h * SB_DH:(h + 1) * SB_DH]


def _sb_in(h1, g, w, B, L, tm):
    n, d = h1.shape
    W = SB_HEADS * SB_DH
    tm = min(tm, L)
    nl = L // tm
    head_spec = pl.BlockSpec((1, SB_HEADS, tm, SB_DH), lambda i: (i // nl, 0, i % nl, 0))
    return pl.pallas_call(
        _sb_in_kernel,
        grid=(n // tm,),
        in_specs=[pl.BlockSpec((tm, d), lambda i: (i, 0)), pl.BlockSpec((1, d), lambda i: (0, 0)),
                  pl.BlockSpec((d, 4 * W), lambda i: (0, 0))],
        out_specs=[pl.BlockSpec((tm, 3 * W), lambda i: (i, 0)), pl.BlockSpec((tm, W), lambda i: (i, 0)),
                   head_spec, head_spec],
        out_shape=[jax.ShapeDtypeStruct((n, 3 * W), BF16), jax.ShapeDtypeStruct((n, W), F32),
                   jax.ShapeDtypeStruct((B, SB_HEADS, L, SB_DH), F32),
                   jax.ShapeDtypeStruct((B, SB_HEADS, L, SB_DH), F32)],
        compiler_params=pltpu.CompilerParams(dimension_semantics=("parallel",),
                                             vmem_limit_bytes=VMEM_LIMIT),
        name="sb_in",
    )(h1, g.reshape(1, d), w)


def _unit_lower_inverse(ps, row, col, C):
    eye = jnp.where(row == col, 1.0, 0.0)
    nb = C // INV_BLOCK
    if nb > 1:
        same = (row // INV_BLOCK) == (col // INV_BLOCK)
        pds = [jnp.where(same, p, 0.0) for p in ps]
    else:
        pds = ps
    xs = [eye + pd for pd in pds]
    pws = [_split(pd) for pd in pds]
    steps = INV_BLOCK.bit_length() - 1
    for _ in range(1, steps):
        pws = [_split(_dot3(pw, pw)) for pw in pws]
        xs = [x + _dot3(pw, _split(x)) for pw, x in zip(pws, xs)]
    if nb == 1:
        return xs
    assert nb <= 4
    xss = [_split(x) for x in xs]
    n1s = [_split(_dot3(xh, _split(p - pd))) for xh, p, pd in zip(xss, ps, pds)]
    n2s = [_split(_dot3(n1, n1)) for n1 in n1s]
    ys = [x + _dot3(n1, xh) for x, n1, xh in zip(xs, n1s, xss)]
    return [y + _dot3(n2, _split(y)) for y, n2 in zip(ys, n2s)]


def _ab_kernel(qk_ref, gv_ref, gz_ref, u_ref, dz_ref, sm_ref, wg_ref, bg_ref, gg_ref, cw_ref, hist_ref,
               alog_ref, dtb_ref, gd_ref, sg0_ref, sd0_ref,
               mix_ref, sg_out_ref, sd_out_ref, conv_out_ref, sg_scr, sd_scr, ubuf, *, C, G):
    c = pl.program_id(1)
    R = G * C
    T8 = SUBLANES
    HG = range(GLA_HEADS)
    HD = range(DN_HEADS)
    GS = range(G)
    rs = [slice(g * C, (g + 1) * C) for g in GS]

    @pl.when(c == 0)
    def _():
        sg_scr[...] = sg0_ref[0]
        sd_scr[...] = sd0_ref[0]
        ubuf[0:T8, :] = hist_ref[0]

    u = u_ref[0]
    ubuf[T8:T8 + R, :] = u
    cw = cw_ref[...]
    conv = u * cw[CONV_W - 1:CONV_W, :]
    for i in range(CONV_W - 1):
        off = T8 - (CONV_W - 1) + i
        conv = conv + ubuf[off:off + R, :] * cw[i:i + 1, :]
    cs = conv * _sigmoid(conv)
    tail = u[R - T8:, :]
    ubuf[0:T8, :] = tail

    @pl.when(c == pl.num_programs(1) - 1)
    def _():
        conv_out_ref[0] = tail

    rr = lax.broadcasted_iota(jnp.int32, (R, R), 0)
    rc = lax.broadcasted_iota(jnp.int32, (R, R), 1)
    same = (rr // C) == (rc // C)
    lblk = jnp.where(jnp.logical_and(same, rr >= rc), 1.0, 0.0).astype(BF16)
    ublk = jnp.where(jnp.logical_and(same, rr <= rc), 1.0, 0.0).astype(BF16)
    row = lax.broadcasted_iota(jnp.int32, (C, C), 0)
    col = lax.broadcasted_iota(jnp.int32, (C, C), 1)
    incl = row >= col
    strict = row > col

    sm = sm_ref[0]
    x = _dot(sm.astype(BF16), wg_ref[...]) + bg_ref[...]
    lgh, lgl = _split(-_softplus(-x) * (1.0 / GLA_TAU))
    b = _dot(lblk, lgh) + _dot(lblk, lgl)
    dh, dl = _split(-jnp.exp(alog_ref[...]) * _softplus(sm + dtb_ref[...]))
    b_all = _dot(lblk, dh) + _dot(lblk, dl)
    b_all_t = _dot_tn(dh, ublk) + _dot_tn(dl, ublk)
    beta_all = _sigmoid(sm)

    qk = qk_ref[0]
    q = qk[:, :_GQ] * (GLA_DK ** -0.5)
    k = qk[:, _GQ:]
    mid = C // 2 - 1
    qe, ke, qd, kd, dlast = [], [], [], [], []
    for g in GS:
        bg_ = b[rs[g], :]
        bmid = bg_[mid:mid + 1, :]
        blast = bg_[C - 1:C, :]
        qe.append((q[rs[g], :] * jnp.exp(bg_ - bmid)).astype(BF16))
        ke.append((k[rs[g], :] * jnp.exp(bmid - bg_)).astype(BF16))
        qd.append((q[rs[g], :] * jnp.exp(bg_)).astype(BF16))
        kd.append((k[rs[g], :] * jnp.exp(blast - bg_)).astype(BF16))
        dlast.append(jnp.exp(blast))
    gks = [slice(h * GLA_DK, (h + 1) * GLA_DK) for h in HG]
    gvs = [slice(h * GLA_DV, (h + 1) * GLA_DV) for h in HG]
    gv = gv_ref[0].astype(BF16)
    att = [[jnp.where(incl, _dot_nt(qe[g][:, gks[h]], ke[g][:, gks[h]]), 0.0).astype(BF16) for h in HG]
           for g in GS]

    dhs = [slice(h * DN_DK, (h + 1) * DN_DK) for h in HD]
    nq = DN_HEADS * DN_DK
    qn = [cs[:, dhs[h]] for h in HD]
    kn = [cs[:, nq + h * DN_DK:nq + (h + 1) * DN_DK] for h in HD]
    vn = [cs[:, 2 * nq + h * DN_DV:2 * nq + (h + 1) * DN_DV] for h in HD]
    qn = [t * lax.rsqrt(jnp.sum(t * t, axis=-1, keepdims=True) + EPS) * (DN_DK ** -0.5) for t in qn]
    kn = [t * lax.rsqrt(jnp.sum(t * t, axis=-1, keepdims=True) + EPS) for t in kn]
    qb = [t.astype(BF16) for t in qn]
    kb = [t.astype(BF16) for t in kn]
    GH = [(g, h) for g in GS for h in HD]
    bcol = {(g, h): b_all[rs[g], SM_DEC + h:SM_DEC + h + 1] for g, h in GH}
    brow = {(g, h): b_all_t[SM_DEC + h:SM_DEC + h + 1, rs[g]] for g, h in GH}
    beta = {(g, h): beta_all[rs[g], SM_BETA + h:SM_BETA + h + 1] for g, h in GH}
    dec = {gh: jnp.where(incl, jnp.exp(jnp.where(incl, bcol[gh] - brow[gh], 0.0)), 0.0) for gh in GH}
    kk = {(g, h): _dot_nt(kb[h][rs[g], :], kb[h][rs[g], :]) for g, h in GH}
    qkd = {(g, h): (_dot_nt(qb[h][rs[g], :], kb[h][rs[g], :]) * dec[(g, h)]).astype(BF16) for g, h in GH}
    ps = [jnp.where(strict, -(beta[gh] * dec[gh] * kk[gh]), 0.0) for gh in GH]
    t_inv = dict(zip(GH, [_split(t) for t in _unit_lower_inverse(ps, row, col, C)]))
    eb = {gh: jnp.exp(bcol[gh]) for gh in GH}
    blast_d = {gh: bcol[gh][C - 1:C, :] for gh in GH}
    kdd = {(g, h): (kn[h][rs[g], :] * jnp.exp(blast_d[(g, h)] - bcol[(g, h)])).astype(BF16) for g, h in GH}

    gz = gz_ref[0]
    dz = dz_ref[0]
    gg = gg_ref[...]
    gd = gd_ref[...]
    for g in GS:
        r = rs[g]
        sd = [sd_scr[h] for h in HD]
        sdb = [s.astype(BF16) for s in sd]
        st = [sg_scr[h] for h in HG]
        ksd = [_dot(kb[h][r, :], sdb[h]) for h in HD]
        qsd = [_dot(qb[h][r, :], sdb[h]) for h in HD]
        gvh = [gv[r, gvs[h]] for h in HG]
        oi = [_dot_nt(qd[g][:, gks[h]], st[h].astype(BF16)) for h in HG]
        sg_new = [st[h] * dlast[g][:, gks[h]] + _dot_tn(gvh[h], kd[g][:, gks[h]]) for h in HG]
        og = [_dot(att[g][h], gvh[h]) + oi[h] for h in HG]
        rhs = [beta[(g, h)] * (vn[h][r, :] - eb[(g, h)] * ksd[h]) for h in HD]
        ub = [_dot3(t_inv[(g, h)], _split(rhs[h])).astype(BF16) for h in HD]
        od = [eb[(g, h)] * qsd[h] + _dot(qkd[(g, h)], ub[h]) for h in HD]
        sd_new = [jnp.exp(blast_d[(g, h)]) * sd[h] + _dot_tn(kdd[(g, h)], ub[h]) for h in HD]
        for h in HG:
            sg_scr[h] = sg_new[h]
            zh = gz[r, gvs[h]]
            mix_ref[0, r, gvs[h]] = (_rms(og[h], gg) * (zh * _sigmoid(zh))).astype(BF16)
        for h in HD:
            sd_scr[h] = sd_new[h]
            zh = dz[r, dhs[h]]
            mix_ref[0, r, _GV + h * DN_DV:_GV + (h + 1) * DN_DV] = (
                _rms(od[h], gd) * (zh * _sigmoid(zh))).astype(BF16)

    @pl.when(c == pl.num_programs(1) - 1)
    def _():
        sg_out_ref[0] = sg_scr[...]
        sd_out_ref[0] = sd_scr[...]


def _ab_mix(p3, wg, bg, gg, cw, hist8, alog, dtb, gd, sg0t, sd0, C, G):
    B, L, _ = p3.shape
    R = G * C
    assert L % R == 0
    blk = lambda w, cb: pl.BlockSpec((1, R, w), lambda b, c, cb=cb: (b, c, cb))
    full = lambda shp: pl.BlockSpec(shp, lambda b, c: (0,) * len(shp))
    sg_spec = pl.BlockSpec((1, GLA_HEADS, GLA_DV, GLA_DK), lambda b, c: (b, 0, 0, 0))
    sd_spec = pl.BlockSpec((1, DN_HEADS, DN_DK, DN_DV), lambda b, c: (b, 0, 0, 0))
    hist_spec = pl.BlockSpec((1, SUBLANES, _DC), lambda b, c: (b, 0, 0))
    return pl.pallas_call(
        functools.partial(_ab_kernel, C=C, G=G),
        grid=(B, L // R),
        in_specs=[blk(2 * _GQ, COL_QK // (2 * _GQ)), blk(_GV, COL_GV // _GV), blk(_GV, COL_GZ // _GV),
                  blk(_DC, COL_DC // _DC), blk(_DW, COL_DZ // _DW), blk(LANES, COL_SM // LANES),
                  full((LANES, _GQ)), full((1, _GQ)), full((1, GLA_DV)), full((CONV_W, _DC)), hist_spec,
                  full((1, LANES)), full((1, LANES)), full((1, DN_DV)), sg_spec, sd_spec],
        out_specs=[pl.BlockSpec((1, R, _GV + _DW), lambda b, c: (b, c, 0)), sg_spec, sd_spec, hist_spec],
        out_shape=[jax.ShapeDtypeStruct((B, L, _GV + _DW), BF16),
                   jax.ShapeDtypeStruct((B, GLA_HEADS, GLA_DV, GLA_DK), F32),
                   jax.ShapeDtypeStruct((B, DN_HEADS, DN_DK, DN_DV), F32),
                   jax.ShapeDtypeStruct((B, SUBLANES, _DC), F32)],
        scratch_shapes=[pltpu.VMEM((GLA_HEADS, GLA_DV, GLA_DK), F32),
                        pltpu.VMEM((DN_HEADS, DN_DK, DN_DV), F32),
                        pltpu.VMEM((R + SUBLANES, _DC), F32)],
        compiler_params=pltpu.CompilerParams(dimension_semantics=("parallel", "arbitrary"),
                                             vmem_limit_bytes=VMEM_LIMIT),
        name="ab_mix",
    )(p3, p3, p3, p3, p3, p3, wg, bg, gg, cw, hist8, alog, dtb, gd, sg0t, sd0)


def _sb_kernel(q_ref, k_ref, v_ref, o_ref, acc, carry, zbuf, zbuf2, ebuf, ztmp, qnorm, kmax, *,
               tq, tk, q_start, slab, nkb):
    i = pl.program_id(2)
    qmin = q_start + i * tq
    jm = qmin // tk
    lane = lax.broadcasted_iota(jnp.int32, (tq, LANES), 1)
    first = lane < SB_DH
    q = q_ref[0]
    zero = jnp.zeros_like(q)
    q2 = jnp.concatenate([jnp.where(first, q, zero), jnp.where(first, zero, q)], axis=0)

    kg = SB_KGROUP if nkb % SB_KGROUP == 0 else 1

    @pl.when(i == 0)
    def _():
        lane_k = lax.broadcasted_iota(jnp.int32, (kg * tk, LANES), 1) < SB_DH

        def kbody(jg, m):
            kf = k_ref[0, pl.ds(pl.multiple_of(jg * (kg * tk), kg * tk), kg * tk), :].astype(F32)
            sq = kf * kf
            n0 = jnp.sqrt(jnp.max(jnp.sum(jnp.where(lane_k, sq, 0.0), axis=1, keepdims=True)))
            n1 = jnp.sqrt(jnp.max(jnp.sum(jnp.where(lane_k, 0.0, sq), axis=1, keepdims=True)))
            m = (jnp.maximum(m[0], n0), jnp.maximum(m[1], n1))
            for t in range(kg):
                kmax[0, jg * kg + t] = m[0]
                kmax[1, jg * kg + t] = m[1]
            return m

        lax.fori_loop(0, nkb // kg, kbody, (jnp.float32(0.0), jnp.float32(0.0)))

    qf = q2.astype(F32)
    qnorm[...] = jnp.sqrt(jnp.sum(qf * qf, axis=1, keepdims=True)) * SB_NORM_SLACK
    kr = lax.broadcasted_iota(jnp.int32, (tk, tk), 0)
    kc = lax.broadcasted_iota(jnp.int32, (tk, tk), 1)
    utri = jnp.where(kr >= kc, 1.0, 0.0).astype(BF16)
    n = 2 * tq // slab
    rows = [slice(t * slab, (t + 1) * slab) for t in range(n)]

    def keys(ref, p):
        j = jnp.clip(jm - p, 0, jm)
        return ref[0, pl.ds(pl.multiple_of(j * tk, tk), tk), :]

    def softplus2(zz):
        neg_abs = lax.bitcast_convert_type(
            lax.bitcast_convert_type(zz, jnp.int32) | jnp.int32(-2 ** 31), F32)
        return jnp.maximum(zz, 0.0) + jnp.log2(1.0 + jnp.exp2(neg_abs))

    kb0, kb1, kb2 = keys(k_ref, 0), keys(k_ref, 1), keys(k_ref, 2)
    vb0 = keys(v_ref, 0)
    vb1 = keys(v_ref, 1)
    vb1 = jnp.where(jm >= 1, vb1, jnp.zeros_like(vb1))
    for t, r in enumerate(rows):
        qpos = qmin + (t * slab + lax.broadcasted_iota(jnp.int32, (slab, tk), 0)) % tq
        kpos = jm * tk + lax.broadcasted_iota(jnp.int32, (slab, tk), 1)
        ebuf[r, :] = _dot_nt(q2[r], kb0) + jnp.where(kpos < qpos, 0.0, SB_NEG)
        ztmp[r, :] = _dot_nt(q2[r], kb1)
        z2 = _dot_nt(q2[r], kb2)
        zbuf[r, :] = z2
        zbuf2[r, :] = z2
    cs0 = [_dot(softplus2(ebuf[r, :]).astype(BF16), utri) for r in rows]
    cs1 = [_dot(softplus2(ztmp[r, :]).astype(BF16), utri) for r in rows]
    for r, c0, c1 in zip(rows, cs0, cs1):
        t0 = c0[:, 0:1]
        w0 = jnp.exp2(ebuf[r, :] - c0)
        w1 = jnp.exp2(ztmp[r, :] - c1 - t0)
        carry[r, :] = t0 + c1[:, 0:1]
        acc[r, :] = _dot(w0.astype(BF16), vb0) + _dot(w1.astype(BF16), vb1)
        ebuf[r, :] = jnp.full((slab, tk), SB_NEG, F32)
    jr0 = jnp.clip(jm - 2, 0, nkb - 1)
    d0 = jnp.max(qnorm[0:tq, :] * kmax[0, jr0] - carry[0:tq, :])
    d1 = jnp.max(qnorm[tq:2 * tq, :] * kmax[1, jr0] - carry[tq:2 * tq, :])
    stop0 = jnp.where(jnp.maximum(d0, d1) < SB_DEAD, 2, jm + 2)

    def body(state):
        it, stop = state
        jr = jnp.clip(jm - it, 0, nkb - 1)
        m0 = jnp.max(qnorm[0:tq, :] * kmax[0, jr] - carry[0:tq, :])
        m1 = jnp.max(qnorm[tq:2 * tq, :] * kmax[1, jr] - carry[tq:2 * tq, :])
        stop = jnp.where(jnp.maximum(m0, m1) < SB_DEAD, jnp.minimum(stop, it + 1), stop)
        kb = keys(k_ref, it + 1)
        vb = keys(v_ref, it - 1)
        for r in rows:
            w = jnp.exp2(ebuf[r, :])
            acc[r, :] += _dot(w.astype(BF16), vb)
            zz = zbuf[r, :]
            neg_abs = lax.bitcast_convert_type(
                lax.bitcast_convert_type(zz, jnp.int32) | jnp.int32(-2 ** 31), F32)
            sp = jnp.maximum(zz, 0.0) + jnp.log2(1.0 + jnp.exp2(neg_abs))
            cs = _dot(sp.astype(BF16), utri)
            z_next = _dot_nt(q2[r], kb)
            ebuf[r, :] = zbuf2[r, :] - cs - carry[r, :]
            carry[r, :] += cs[:, 0:1]
            zbuf[r, :] = z_next
            zbuf2[r, :] = z_next
        return it + 1, stop

    lax.while_loop(lambda state: state[0] < state[1], body, (jnp.int32(2), stop0))
    o_ref[0] = jnp.where(first, acc[0:tq, :], acc[tq:2 * tq, :])


def _sb_attention(q_arr, q_cb, k_arr, k_cb, v_arr, v_cb, *, tq, tk, q_start, name):
    B, Tq = q_arr.shape[:2]
    Tk = k_arr.shape[1]
    assert Tq % tq == 0 and Tk % tk == 0 and tk % tq == 0 and q_start % tk == 0
    assert q_start + Tq <= Tk
    hp = SB_HEADS // 2
    kv_spec = lambda cb0: pl.BlockSpec((1, Tk, LANES), lambda b, h, i, cb0=cb0: (b, 0, cb0 + h))
    slab = min(SB_SLAB, 2 * tq)
    nkb = Tk // tk
    return pl.pallas_call(
        functools.partial(_sb_kernel, tq=tq, tk=tk, q_start=q_start, slab=slab, nkb=nkb),
        grid=(B, hp, Tq // tq),
        in_specs=[pl.BlockSpec((1, tq, LANES), lambda b, h, i: (b, i, q_cb + h)),
                  kv_spec(k_cb), kv_spec(v_cb)],
        out_specs=pl.BlockSpec((1, tq, LANES), lambda b, h, i: (b, i, h)),
        out_shape=jax.ShapeDtypeStruct((B, Tq, SB_HEADS * SB_DH), F32),
        scratch_shapes=[pltpu.VMEM((2 * tq, LANES), F32), pltpu.VMEM((2 * tq, 1), F32),
                        pltpu.VMEM((2 * tq, tk), F32), pltpu.VMEM((2 * tq, tk), F32),
                        pltpu.VMEM((2 * tq, tk), F32), pltpu.VMEM((2 * tq, tk), F32),
                        pltpu.VMEM((2 * tq, 1), F32), pltpu.SMEM((2, nkb), F32)],
        compiler_params=pltpu.CompilerParams(dimension_semantics=("parallel", "parallel", "arbitrary"),
                                             vmem_limit_bytes=VMEM_LIMIT),
        name=name,
    )(q_arr, k_arr, v_arr)


def _from_heads(t):
    b, h, l, d = t.shape
    return t.transpose(0, 2, 1, 3).reshape(b, l, h * d)


def _pad_lanes(vec, offset):
    out = jnp.zeros((1, LANES), F32)
    return lax.dynamic_update_slice(out, vec.reshape(1, -1).astype(F32), (0, offset))


def _group(x, s_gla, s_dn, conv_hist, cache_k, cache_v, C, tq, w):
    B, L, D = x.shape
    n = B * L
    x2 = x.reshape(n, D)
    tm = 512

    p = _linear([(x2, 0)], w["ab_w_in"], norm=w["ab_norm"], tm=256, name="ab_in")
    p3 = p.reshape(B, L, AB_COLS)
    hist8 = jnp.pad(conv_hist, ((0, 0), (SUBLANES - (CONV_W - 1), 0), (0, 0)))
    mix, s_gla_t, s_dn_new, conv8 = _ab_mix(
        p3, w["gla_w_gate"], w["gla_b_gate"], w["gla_out_norm"], w["dn_conv_w"], hist8, w["dn_a_log"],
        w["dn_dt_bias"], w["dn_out_norm"], jnp.swapaxes(s_gla, -1, -2), s_dn, C, min(AB_GROUP, L // C))
    h1 = _linear([(mix.reshape(n, _GV + _DW), 0)], w["ab_w_out"], res=(x2, 0), tm=tm, name="ab_out")

    W = SB_HEADS * SB_DH
    qkv, z, k_heads, v_heads = _sb_in(h1, w["sb_norm"], w["sb_w_in"], B, L, 256)
    qkv3 = qkv.reshape(B, L, 3 * W)
    nb = W // LANES
    if cache_k is None:
        o = _sb_attention(qkv3, 0, qkv3, nb, qkv3, 2 * nb, tq=tq, tk=tq, q_start=0, name="sb_prompt")
    else:
        past = cache_k.shape[2]
        tk = 256
        tk_pad = -(-(past + L) // tk) * tk
        pad = ((0, 0), (0, tk_pad - past - L), (0, 0))
        k_all = jnp.pad(jnp.concatenate([_from_heads(cache_k).astype(BF16), qkv3[:, :, W:2 * W]], axis=1), pad)
        v_all = jnp.pad(jnp.concatenate([_from_heads(cache_v).astype(BF16), qkv3[:, :, 2 * W:]], axis=1), pad)
        o = _sb_attention(qkv3, 0, k_all, 0, v_all, 0, tq=tq, tk=tk, q_start=past, name="sb_sample")
    y = _linear([(o.reshape(n, W), 0)], w["sb_w_out"], gate=(z, 0), res=(h1, 0),
                post=w["final_norm"], tm=tm, name="sb_out")
    return (y.reshape(B, L, D), jnp.swapaxes(s_gla_t, -1, -2), s_dn_new,
            conv8[:, SUBLANES - (CONV_W - 1):], k_heads, v_heads)


def kernel(x_prompt, x_sample, state_gla, state_delta, state_conv, cache_k, cache_v, ab_norm, ab_w_in,
           gla_w_gate, gla_b_gate, gla_out_norm, dn_conv_w, dn_a_log, dn_dt_bias, dn_out_norm, ab_w_out,
           sb_norm, sb_w_in, sb_w_out, final_norm):
    assert ab_w_in.shape[0] == 1 and sb_w_in.shape[0] == 1, "one (A|B) layer followed by one C layer"
    D = x_prompt.shape[-1]
    wi = ab_w_in[0]
    offs, acc = [], 0
    for s in (_GQ, _GQ, _GV, GLA_RANK, _GV, _DC, DN_HEADS, DN_HEADS, _DW):
        offs.append((acc, acc + s))
        acc += s
    gq, gk, gv, gr, gz, dqkv, dbeta, da, dz = [wi[:, a:b] for a, b in offs]
    small = jnp.concatenate([gr, dbeta, da], axis=1)
    w_ab = jnp.concatenate(
        [gq, gk, gv, gz, dqkv, dz, small,
         jnp.zeros((D, AB_COLS - COL_SM - small.shape[1]), wi.dtype)], axis=1).astype(BF16)
    wg = jnp.zeros((LANES, _GQ), F32).at[SM_RANK:SM_RANK + GLA_RANK].set(gla_w_gate[0]).astype(BF16)
    w = {
        "ab_norm": ab_norm[0], "ab_w_in": w_ab, "gla_w_gate": wg,
        "gla_b_gate": gla_b_gate[0].reshape(1, _GQ), "gla_out_norm": gla_out_norm[0].reshape(1, GLA_DV),
        "dn_conv_w": dn_conv_w[0], "dn_a_log": _pad_lanes(dn_a_log[0], SM_DEC),
        "dn_dt_bias": _pad_lanes(dn_dt_bias[0], SM_DEC), "dn_out_norm": dn_out_norm[0].reshape(1, DN_DV),
        "ab_w_out": ab_w_out[0].astype(BF16), "sb_norm": sb_norm[0], "sb_w_in": sb_w_in[0].astype(BF16),
        "sb_w_out": sb_w_out[0].astype(BF16), "final_norm": final_norm,
    }
    bp = x_prompt.shape[0]
    yp, gp, dp, cp, kp, vp = _group(
        x_prompt, jnp.zeros((bp, GLA_HEADS, GLA_DK, GLA_DV), F32), jnp.zeros((bp, DN_HEADS, DN_DK, DN_DV), F32),
        jnp.zeros((bp, CONV_W - 1, _DC), F32), None, None, CHUNK, 256, w)
    ys, gs, ds, cs, ks, vs = _group(
        x_sample, state_gla[0], state_delta[0], state_conv[0], cache_k[0], cache_v[0],
        x_sample.shape[1], x_sample.shape[1], w)
    st = lambda a: a[None]
    return (yp, ys, st(gp), st(gs), st(dp), st(ds), st(cp), st(cs), st(kp), st(ks), st(vp), st(vs))
```

```python
import functools

import jax
import jax.numpy as jnp
from jax import lax
from jax.experimental import pallas as pl
from jax.experimental.pallas import tpu as pltpu

F32 = jnp.float32
BF16 = jnp.bfloat16

EPS = 1e-6
CHUNK = 64
GLA_HEADS = 4
GLA_DK = 64
GLA_DV = 128
GLA_RANK = 16
GLA_TAU = 16.0
DN_HEADS = 4
DN_DK = 128
DN_DV = 128
CONV_W = 4
SB_HEADS = 16
SB_DH = 64

LANES = 128
SUBLANES = 8
INV_BLOCK = 16
AB_GROUP = 4
SB_SLAB = 256
SB_NEG = -1e30
SB_DEAD = -160.0
SB_KGROUP = 8
SB_NORM_SLACK = 1.001
LOG2E = 1.4426950408889634
VMEM_LIMIT = 56 * 1024 * 1024

_GQ = GLA_HEADS * GLA_DK
_GV = GLA_HEADS * GLA_DV
_DC = 2 * DN_HEADS * DN_DK + DN_HEADS * DN_DV
_DW = DN_HEADS * DN_DV
COL_QK = 0
COL_GV = 2 * _GQ
COL_GZ = COL_GV + _GV
COL_DC = COL_GZ + _GV
COL_DZ = COL_DC + _DC
COL_SM = COL_DZ + _DW
AB_COLS = 3840
SM_RANK = 0
SM_BETA = GLA_RANK
SM_DEC = GLA_RANK + DN_HEADS


def _sigmoid(x):
    return 1.0 / (1.0 + jnp.exp(-x))


def _softplus(x):
    return jnp.maximum(x, 0.0) + jnp.log(1.0 + jnp.exp(-jnp.abs(x)))


def _split(a):
    hi = a.astype(BF16)
    lo = (a - hi.astype(F32)).astype(BF16)
    return hi, lo


def _dot(a, b):
    return jnp.dot(a, b, preferred_element_type=F32)


def _dot_nt(a, b):
    return lax.dot_general(a, b, (((1,), (1,)), ((), ())), preferred_element_type=F32)


def _dot_tn(a, b):
    return lax.dot_general(a, b, (((0,), (0,)), ((), ())), preferred_element_type=F32)


def _dot3(a, b):
    ah, al = a
    bh, bl = b
    return _dot(ah, bh) + _dot(ah, bl) + _dot(al, bh)


def _rms(x, g):
    return x * lax.rsqrt(jnp.mean(x * x, axis=-1, keepdims=True) + EPS) * g


def _linear_kernel(*refs, n_x, has_norm, has_gate, has_res, has_post):
    refs = list(refs)
    xs = [refs.pop(0) for _ in range(n_x)]
    g_ref = refs.pop(0) if has_norm else None
    z_ref = refs.pop(0) if has_gate else None
    w_ref = refs.pop(0)
    r_ref = refs.pop(0) if has_res else None
    p_ref = refs.pop(0) if has_post else None
    o_ref = refs.pop(0)
    x = xs[0][...] if n_x == 1 else jnp.concatenate([r[...] for r in xs], axis=-1)
    if has_norm:
        x = _rms(x, g_ref[...])
    if has_gate:
        z = z_ref[...]
        x = x * (z * _sigmoid(z))
    y = _dot(x.astype(BF16), w_ref[...])
    if has_res:
        y = y + r_ref[...]
    if has_post:
        y = _rms(y, p_ref[...])
    o_ref[...] = y


def _linear(xs, w, *, norm=None, gate=None, res=None, post=None, tm, name):
    n = xs[0][0].shape[0]
    k, m = w.shape
    tm = min(tm, n)
    assert n % tm == 0
    in_specs, args = [], []
    kx = k // len(xs)
    for arr, cb in xs:
        in_specs.append(pl.BlockSpec((tm, kx), lambda i, cb=cb: (i, cb)))
        args.append(arr)
    if norm is not None:
        in_specs.append(pl.BlockSpec((1, k), lambda i: (0, 0)))
        args.append(norm.reshape(1, k))
    if gate is not None:
        in_specs.append(pl.BlockSpec((tm, k), lambda i, cb=gate[1]: (i, cb)))
        args.append(gate[0])
    in_specs.append(pl.BlockSpec((k, m), lambda i: (0, 0)))
    args.append(w)
    if res is not None:
        in_specs.append(pl.BlockSpec((tm, m), lambda i, cb=res[1]: (i, cb)))
        args.append(res[0])
    if post is not None:
        in_specs.append(pl.BlockSpec((1, m), lambda i: (0, 0)))
        args.append(post.reshape(1, m))
    kern = functools.partial(_linear_kernel, n_x=len(xs), has_norm=norm is not None,
                             has_gate=gate is not None, has_res=res is not None,
                             has_post=post is not None)
    return pl.pallas_call(
        kern,
        grid=(n // tm,),
        in_specs=in_specs,
        out_specs=pl.BlockSpec((tm, m), lambda i: (i, 0)),
        out_shape=jax.ShapeDtypeStruct((n, m), F32),
        compiler_params=pltpu.CompilerParams(dimension_semantics=("parallel",),
                                             vmem_limit_bytes=VMEM_LIMIT),
        name=name,
    )(*args)


def _sb_in_kernel(x_ref, g_ref, w_ref, qkv_ref, z_ref, kh_ref, vh_ref):
    W = SB_HEADS * SB_DH
    y = _dot(_rms(x_ref[...], g_ref[...]).astype(BF16), w_ref[...])
    k = y[:, W:2 * W]
    v = y[:, 2 * W:3 * W]
    qkv_ref[:, 0:W] = (y[:, 0:W] * (SB_DH ** -0.5 * LOG2E)).astype(BF16)
    qkv_ref[:, W:2 * W] = k.astype(BF16)
    qkv_ref[:, 2 * W:3 * W] = v.astype(BF16)
    z_ref[...] = y[:, 3 * W:].astype(BF16)
    for h in range(SB_HEADS):
        kh_ref[0, h] = k[:, h * SB_DH:(h + 1) * SB_DH]
        vh_ref[0, h] = v[:, h * SB_DH:(h + 1) * SB_DH]


def _sb_in(h1, g, w, B, L, tm):
    n, d = h1.shape
    W = SB_HEADS * SB_DH
    tm = min(tm, L)
    nl = L // tm
    head_spec = pl.BlockSpec((1, SB_HEADS, tm, SB_DH), lambda i: (i // nl, 0, i % nl, 0))
    return pl.pallas_call(
        _sb_in_kernel,
        grid=(n // tm,),
        in_specs=[pl.BlockSpec((tm, d), lambda i: (i, 0)), pl.BlockSpec((1, d), lambda i: (0, 0)),
                  pl.BlockSpec((d, 4 * W), lambda i: (0, 0))],
        out_specs=[pl.BlockSpec((tm, 3 * W), lambda i: (i, 0)), pl.BlockSpec((tm, W), lambda i: (i, 0)),
                   head_spec, head_spec],
        out_shape=[jax.ShapeDtypeStruct((n, 3 * W), BF16), jax.ShapeDtypeStruct((n, W), BF16),
                   jax.ShapeDtypeStruct((B, SB_HEADS, L, SB_DH), F32),
                   jax.ShapeDtypeStruct((B, SB_HEADS, L, SB_DH), F32)],
        compiler_params=pltpu.CompilerParams(dimension_semantics=("parallel",),
                                             vmem_limit_bytes=VMEM_LIMIT),
        name="sb_in",
    )(h1, g.reshape(1, d), w)


def _unit_lower_inverse(ps, row, col, C):
    eye = jnp.where(row == col, 1.0, 0.0)
    nb = C // INV_BLOCK
    if nb > 1:
        same = (row // INV_BLOCK) == (col // INV_BLOCK)
        pds = [jnp.where(same, p, 0.0) for p in ps]
    else:
        pds = ps
    xs = [eye + pd for pd in pds]
    pws = [_split(pd) for pd in pds]
    steps = INV_BLOCK.bit_length() - 1
    for _ in range(1, steps):
        pws = [_split(_dot3(pw, pw)) for pw in pws]
        xs = [x + _dot3(pw, _split(x)) for pw, x in zip(pws, xs)]
    if nb == 1:
        return xs
    assert nb <= 4
    xss = [_split(x) for x in xs]
    n1s = [_split(_dot3(xh, _split(p - pd))) for xh, p, pd in zip(xss, ps, pds)]
    n2s = [_split(_dot3(n1, n1)) for n1 in n1s]
    ys = [x + _dot3(n1, xh) for x, n1, xh in zip(xs, n1s, xss)]
    return [y + _dot3(n2, _split(y)) for y, n2 in zip(ys, n2s)]


def _ab_kernel(qk_ref, gv_ref, gz_ref, u_ref, dz_ref, sm_ref, wg_ref, bg_ref, gg_ref, cw_ref, hist_ref,
               alog_ref, dtb_ref, gd_ref, sg0_ref, sd0_ref,
               mix_ref, sg_out_ref, sd_out_ref, conv_out_ref, sg_scr, sd_scr, ubuf, *, C, G):
    c = pl.program_id(1)
    R = G * C
    T8 = SUBLANES
    HG = range(GLA_HEADS)
    HD = range(DN_HEADS)
    GS = range(G)
    rs = [slice(g * C, (g + 1) * C) for g in GS]

    @pl.when(c == 0)
    def _():
        sg_scr[...] = sg0_ref[0]
        sd_scr[...] = sd0_ref[0]
        ubuf[0:T8, :] = hist_ref[0]

    u = u_ref[0]
    ubuf[T8:T8 + R, :] = u
    cw = cw_ref[...]
    conv = u * cw[CONV_W - 1:CONV_W, :]
    for i in range(CONV_W - 1):
        off = T8 - (CONV_W - 1) + i
        conv = conv + ubuf[off:off + R, :] * cw[i:i + 1, :]
    cs = conv * _sigmoid(conv)
    tail = u[R - T8:, :]
    ubuf[0:T8, :] = tail

    @pl.when(c == pl.num_programs(1) - 1)
    def _():
        conv_out_ref[0] = tail

    rr = lax.broadcasted_iota(jnp.int32, (R, R), 0)
    rc = lax.broadcasted_iota(jnp.int32, (R, R), 1)
    same = (rr // C) == (rc // C)
    lblk = jnp.where(jnp.logical_and(same, rr >= rc), 1.0, 0.0).astype(BF16)
    ublk = jnp.where(jnp.logical_and(same, rr <= rc), 1.0, 0.0).astype(BF16)
    row = lax.broadcasted_iota(jnp.int32, (C, C), 0)
    col = lax.broadcasted_iota(jnp.int32, (C, C), 1)
    incl = row >= col
    strict = row > col

    sm = sm_ref[0]
    x = _dot(sm.astype(BF16), wg_ref[...]) + bg_ref[...]
    lgh, lgl = _split(-_softplus(-x) * (1.0 / GLA_TAU))
    b = _dot(lblk, lgh) + _dot(lblk, lgl)
    dh, dl = _split(-jnp.exp(alog_ref[...]) * _softplus(sm + dtb_ref[...]))
    b_all = _dot(lblk, dh) + _dot(lblk, dl)
    b_all_t = _dot_tn(dh, ublk) + _dot_tn(dl, ublk)
    beta_all = _sigmoid(sm)

    qk = qk_ref[0]
    q = qk[:, :_GQ] * (GLA_DK ** -0.5)
    k = qk[:, _GQ:]
    mid = C // 2 - 1
    qe, ke, qd, kd, dlast = [], [], [], [], []
    for g in GS:
        bg_ = b[rs[g], :]
        bmid = bg_[mid:mid + 1, :]
        blast = bg_[C - 1:C, :]
        qe.append((q[rs[g], :] * jnp.exp(bg_ - bmid)).astype(BF16))
        ke.append((k[rs[g], :] * jnp.exp(bmid - bg_)).astype(BF16))
        qd.append((q[rs[g], :] * jnp.exp(bg_)).astype(BF16))
        kd.append((k[rs[g], :] * jnp.exp(blast - bg_)).astype(BF16))
        dlast.append(jnp.exp(blast))
    gks = [slice(h * GLA_DK, (h + 1) * GLA_DK) for h in HG]
    gvs = [slice(h * GLA_DV, (h + 1) * GLA_DV) for h in HG]
    gv = gv_ref[0].astype(BF16)
    att = [[jnp.where(incl, _dot_nt(qe[g][:, gks[h]], ke[g][:, gks[h]]), 0.0).astype(BF16) for h in HG]
           for g in GS]

    dhs = [slice(h * DN_DK, (h + 1) * DN_DK) for h in HD]
    nq = DN_HEADS * DN_DK
    qn = [cs[:, dhs[h]] for h in HD]
    kn = [cs[:, nq + h * DN_DK:nq + (h + 1) * DN_DK] for h in HD]
    vn = [cs[:, 2 * nq + h * DN_DV:2 * nq + (h + 1) * DN_DV] for h in HD]
    qn = [t * lax.rsqrt(jnp.sum(t * t, axis=-1, keepdims=True) + EPS) * (DN_DK ** -0.5) for t in qn]
    kn = [t * lax.rsqrt(jnp.sum(t * t, axis=-1, keepdims=True) + EPS) for t in kn]
    qb = [t.astype(BF16) for t in qn]
    kb = [t.astype(BF16) for t in kn]
    GH = [(g, h) for g in GS for h in HD]
    bcol = {(g, h): b_all[rs[g], SM_DEC + h:SM_DEC + h + 1] for g, h in GH}
    brow = {(g, h): b_all_t[SM_DEC + h:SM_DEC + h + 1, rs[g]] for g, h in GH}
    beta = {(g, h): beta_all[rs[g], SM_BETA + h:SM_BETA + h + 1] for g, h in GH}
    dec = {gh: jnp.where(incl, jnp.exp(jnp.where(incl, bcol[gh] - brow[gh], 0.0)), 0.0) for gh in GH}
    kk = {(g, h): _dot_nt(kb[h][rs[g], :], kb[h][rs[g], :]) for g, h in GH}
    qkd = {(g, h): (_dot_nt(qb[h][rs[g], :], kb[h][rs[g], :]) * dec[(g, h)]).astype(BF16) for g, h in GH}
    ps = [jnp.where(strict, -(beta[gh] * dec[gh] * kk[gh]), 0.0) for gh in GH]
    t_inv = dict(zip(GH, [_split(t) for t in _unit_lower_inverse(ps, row, col, C)]))
    eb = {gh: jnp.exp(bcol[gh]) for gh in GH}
    blast_d = {gh: bcol[gh][C - 1:C, :] for gh in GH}
    kdd = {(g, h): (kn[h][rs[g], :] * jnp.exp(blast_d[(g, h)] - bcol[(g, h)])).astype(BF16) for g, h in GH}

    gz = gz_ref[0]
    dz = dz_ref[0]
    gg = gg_ref[...]
    gd = gd_ref[...]
    for g in GS:
        r = rs[g]
        sd = [sd_scr[h] for h in HD]
        sdb = [s.astype(BF16) for s in sd]
        st = [sg_scr[h] for h in HG]
        ksd = [_dot(kb[h][r, :], sdb[h]) for h in HD]
        qsd = [_dot(qb[h][r, :], sdb[h]) for h in HD]
        gvh = [gv[r, gvs[h]] for h in HG]
        oi = [_dot_nt(qd[g][:, gks[h]], st[h].astype(BF16)) for h in HG]
        sg_new = [st[h] * dlast[g][:, gks[h]] + _dot_tn(gvh[h], kd[g][:, gks[h]]) for h in HG]
        og = [_dot(att[g][h], gvh[h]) + oi[h] for h in HG]
        rhs = [beta[(g, h)] * (vn[h][r, :] - eb[(g, h)] * ksd[h]) for h in HD]
        ub = [_dot3(t_inv[(g, h)], _split(rhs[h])).astype(BF16) for h in HD]
        od = [eb[(g, h)] * qsd[h] + _dot(qkd[(g, h)], ub[h]) for h in HD]
        sd_new = [jnp.exp(blast_d[(g, h)]) * sd[h] + _dot_tn(kdd[(g, h)], ub[h]) for h in HD]
        for h in HG:
            sg_scr[h] = sg_new[h]
            zh = gz[r, gvs[h]]
            mix_ref[0, r, gvs[h]] = (_rms(og[h], gg) * (zh * _sigmoid(zh))).astype(BF16)
        for h in HD:
            sd_scr[h] = sd_new[h]
            zh = dz[r, dhs[h]]
            mix_ref[0, r, _GV + h * DN_DV:_GV + (h + 1) * DN_DV] = (
                _rms(od[h], gd) * (zh * _sigmoid(zh))).astype(BF16)

    @pl.when(c == pl.num_programs(1) - 1)
    def _():
        sg_out_ref[0] = sg_scr[...]
        sd_out_ref[0] = sd_scr[...]


def _ab_mix(p3, wg, bg, gg, cw, hist8, alog, dtb, gd, sg0t, sd0, C, G):
    B, L, _ = p3.shape
    R = G * C
    assert L % R == 0
    blk = lambda w, cb: pl.BlockSpec((1, R, w), lambda b, c, cb=cb: (b, c, cb))
    full = lambda shp: pl.BlockSpec(shp, lambda b, c: (0,) * len(shp))
    sg_spec = pl.BlockSpec((1, GLA_HEADS, GLA_DV, GLA_DK), lambda b, c: (b, 0, 0, 0))
    sd_spec = pl.BlockSpec((1, DN_HEADS, DN_DK, DN_DV), lambda b, c: (b, 0, 0, 0))
    hist_spec = pl.BlockSpec((1, SUBLANES, _DC), lambda b, c: (b, 0, 0))
    return pl.pallas_call(
        functools.partial(_ab_kernel, C=C, G=G),
        grid=(B, L // R),
        in_specs=[blk(2 * _GQ, COL_QK // (2 * _GQ)), blk(_GV, COL_GV // _GV), blk(_GV, COL_GZ // _GV),
                  blk(_DC, COL_DC // _DC), blk(_DW, COL_DZ // _DW), blk(LANES, COL_SM // LANES),
                  full((LANES, _GQ)), full((1, _GQ)), full((1, GLA_DV)), full((CONV_W, _DC)), hist_spec,
                  full((1, LANES)), full((1, LANES)), full((1, DN_DV)), sg_spec, sd_spec],
        out_specs=[pl.BlockSpec((1, R, _GV + _DW), lambda b, c: (b, c, 0)), sg_spec, sd_spec, hist_spec],
        out_shape=[jax.ShapeDtypeStruct((B, L, _GV + _DW), BF16),
                   jax.ShapeDtypeStruct((B, GLA_HEADS, GLA_DV, GLA_DK), F32),
                   jax.ShapeDtypeStruct((B, DN_HEADS, DN_DK, DN_DV), F32),
                   jax.ShapeDtypeStruct((B, SUBLANES, _DC), F32)],
        scratch_shapes=[pltpu.VMEM((GLA_HEADS, GLA_DV, GLA_DK), F32),
                        pltpu.VMEM((DN_HEADS, DN_DK, DN_DV), F32),
                        pltpu.VMEM((R + SUBLANES, _DC), F32)],
        compiler_params=pltpu.CompilerParams(dimension_semantics=("parallel", "arbitrary"),
                                             vmem_limit_bytes=VMEM_LIMIT),
        name="ab_mix",
    )(p3, p3, p3, p3, p3, p3, wg, bg, gg, cw, hist8, alog, dtb, gd, sg0t, sd0)


def _sb_kernel(q_ref, k_ref, v_ref, o_ref, acc, carry, zbuf, zbuf2, ebuf, ztmp, qnorm, kmax, *,
               tq, tk, q_start, slab, nkb):
    i = pl.program_id(2)
    qmin = q_start + i * tq
    jm = qmin // tk
    lane = lax.broadcasted_iota(jnp.int32, (tq, LANES), 1)
    first = lane < SB_DH
    q = q_ref[0]
    zero = jnp.zeros_like(q)
    q2 = jnp.concatenate([jnp.where(first, q, zero), jnp.where(first, zero, q)], axis=0)

    kg = SB_KGROUP if nkb % SB_KGROUP == 0 else 1

    @pl.when(i == 0)
    def _():
        lane_k = lax.broadcasted_iota(jnp.int32, (kg * tk, LANES), 1) < SB_DH

        def kbody(jg, m):
            kf = k_ref[0, pl.ds(pl.multiple_of(jg * (kg * tk), kg * tk), kg * tk), :].astype(F32)
            sq = kf * kf
            n0 = jnp.sqrt(jnp.max(jnp.sum(jnp.where(lane_k, sq, 0.0), axis=1, keepdims=True)))
            n1 = jnp.sqrt(jnp.max(jnp.sum(jnp.where(lane_k, 0.0, sq), axis=1, keepdims=True)))
            m = (jnp.maximum(m[0], n0), jnp.maximum(m[1], n1))
            for t in range(kg):
                kmax[0, jg * kg + t] = m[0]
                kmax[1, jg * kg + t] = m[1]
            return m

        lax.fori_loop(0, nkb // kg, kbody, (jnp.float32(0.0), jnp.float32(0.0)))

    qf = q2.astype(F32)
    qnorm[...] = jnp.sqrt(jnp.sum(qf * qf, axis=1, keepdims=True)) * SB_NORM_SLACK
    kr = lax.broadcasted_iota(jnp.int32, (tk, tk), 0)
    kc = lax.broadcasted_iota(jnp.int32, (tk, tk), 1)
    utri = jnp.where(kr >= kc, 1.0, 0.0).astype(BF16)
    n = 2 * tq // slab
    rows = [slice(t * slab, (t + 1) * slab) for t in range(n)]

    def keys(ref, p):
        j = jnp.clip(jm - p, 0, jm)
        return ref[0, pl.ds(pl.multiple_of(j * tk, tk), tk), :]

    def softplus2(zz):
        neg_abs = lax.bitcast_convert_type(
            lax.bitcast_convert_type(zz, jnp.int32) | jnp.int32(-2 ** 31), F32)
        return jnp.maximum(zz, 0.0) + jnp.log2(1.0 + jnp.exp2(neg_abs))

    kb0, kb1, kb2 = keys(k_ref, 0), keys(k_ref, 1), keys(k_ref, 2)
    vb0 = keys(v_ref, 0)
    vb1 = keys(v_ref, 1)
    vb1 = jnp.where(jm >= 1, vb1, jnp.zeros_like(vb1))
    for t, r in enumerate(rows):
        qpos = qmin + (t * slab + lax.broadcasted_iota(jnp.int32, (slab, tk), 0)) % tq
        kpos = jm * tk + lax.broadcasted_iota(jnp.int32, (slab, tk), 1)
        ebuf[r, :] = _dot_nt(q2[r], kb0) + jnp.where(kpos < qpos, 0.0, SB_NEG)
        ztmp[r, :] = _dot_nt(q2[r], kb1)
        z2 = _dot_nt(q2[r], kb2)
        zbuf[r, :] = z2
        zbuf2[r, :] = z2
    cs0 = [_dot(softplus2(ebuf[r, :]).astype(BF16), utri) for r in rows]
    cs1 = [_dot(softplus2(ztmp[r, :]).astype(BF16), utri) for r in rows]
    for r, c0, c1 in zip(rows, cs0, cs1):
        t0 = c0[:, 0:1]
        w0 = jnp.exp2(ebuf[r, :] - c0)
        w1 = jnp.exp2(ztmp[r, :] - c1 - t0)
        carry[r, :] = t0 + c1[:, 0:1]
        acc[r, :] = _dot(w0.astype(BF16), vb0) + _dot(w1.astype(BF16), vb1)
        ebuf[r, :] = jnp.full((slab, tk), SB_NEG, F32)
    jr0 = jnp.clip(jm - 2, 0, nkb - 1)
    d0 = jnp.max(qnorm[0:tq, :] * kmax[0, jr0] - carry[0:tq, :])
    d1 = jnp.max(qnorm[tq:2 * tq, :] * kmax[1, jr0] - carry[tq:2 * tq, :])
    stop0 = jnp.where(jnp.maximum(d0, d1) < SB_DEAD, 2, jm + 2)

    def body(state):
        it, stop = state
        jr = jnp.clip(jm - it, 0, nkb - 1)
        m0 = jnp.max(qnorm[0:tq, :] * kmax[0, jr] - carry[0:tq, :])
        m1 = jnp.max(qnorm[tq:2 * tq, :] * kmax[1, jr] - carry[tq:2 * tq, :])
        stop = jnp.where(jnp.maximum(m0, m1) < SB_DEAD, jnp.minimum(stop, it + 1), stop)
        kb = keys(k_ref, it + 1)
        vb = keys(v_ref, it - 1)
        for r in rows:
            w = jnp.exp2(ebuf[r, :])
            acc[r, :] += _dot(w.astype(BF16), vb)
            zz = zbuf[r, :]
            neg_abs = lax.bitcast_convert_type(
                lax.bitcast_convert_type(zz, jnp.int32) | jnp.int32(-2 ** 31), F32)
            sp = jnp.maximum(zz, 0.0) + jnp.log2(1.0 + jnp.exp2(neg_abs))
            cs = _dot(sp.astype(BF16), utri)
            z_next = _dot_nt(q2[r], kb)
            ebuf[r, :] = zbuf2[r, :] - cs - carry[r, :]
            carry[r, :] += cs[:, 0:1]
            zbuf[r, :] = z_next
            zbuf2[r, :] = z_next
        return it + 1, stop

    lax.while_loop(lambda state: state[0] < state[1], body, (jnp.int32(2), stop0))
    o_ref[0] = jnp.where(first, acc[0:tq, :], acc[tq:2 * tq, :]).astype(BF16)


def _sb_attention(q_arr, q_cb, k_arr, k_cb, v_arr, v_cb, *, tq, tk, q_start, name):
    B, Tq = q_arr.shape[:2]
    Tk = k_arr.shape[1]
    assert Tq % tq == 0 and Tk % tk == 0 and tk % tq == 0 and q_start % tk == 0
    assert q_start + Tq <= Tk
    hp = SB_HEADS // 2
    kv_spec = lambda cb0: pl.BlockSpec((1, Tk, LANES), lambda b, h, i, cb0=cb0: (b, 0, cb0 + h))
    slab = min(SB_SLAB, 2 * tq)
    nkb = Tk // tk
    return pl.pallas_call(
        functools.partial(_sb_kernel, tq=tq, tk=tk, q_start=q_start, slab=slab, nkb=nkb),
        grid=(B, hp, Tq // tq),
        in_specs=[pl.BlockSpec((1, tq, LANES), lambda b, h, i: (b, i, q_cb + h)),
                  kv_spec(k_cb), kv_spec(v_cb)],
        out_specs=pl.BlockSpec((1, tq, LANES), lambda b, h, i: (b, i, h)),
        out_shape=jax.ShapeDtypeStruct((B, Tq, SB_HEADS * SB_DH), BF16),
        scratch_shapes=[pltpu.VMEM((2 * tq, LANES), F32), pltpu.VMEM((2 * tq, 1), F32),
                        pltpu.VMEM((2 * tq, tk), F32), pltpu.VMEM((2 * tq, tk), F32),
                        pltpu.VMEM((2 * tq, tk), F32), pltpu.VMEM((2 * tq, tk), F32),
                        pltpu.VMEM((2 * tq, 1), F32), pltpu.SMEM((2, nkb), F32)],
        compiler_params=pltpu.CompilerParams(dimension_semantics=("parallel", "parallel", "arbitrary"),
                                             vmem_limit_bytes=VMEM_LIMIT),
        name=name,
    )(q_arr, k_arr, v_arr)


def _from_heads(t):
    b, h, l, d = t.shape
    return t.transpose(0, 2, 1, 3).reshape(b, l, h * d)


def _pad_lanes(vec, offset):
    out = jnp.zeros((1, LANES), F32)
    return lax.dynamic_update_slice(out, vec.reshape(1, -1).astype(F32), (0, offset))


def _group(x, s_gla, s_dn, conv_hist, cache_k, cache_v, C, tq, w):
    B, L, D = x.shape
    n = B * L
    x2 = x.reshape(n, D)
    tm = 512

    p = _linear([(x2, 0)], w["ab_w_in"], norm=w["ab_norm"], tm=256, name="ab_in")
    p3 = p.reshape(B, L, AB_COLS)
    hist8 = jnp.pad(conv_hist, ((0, 0), (SUBLANES - (CONV_W - 1), 0), (0, 0)))
    mix, s_gla_t, s_dn_new, conv8 = _ab_mix(
        p3, w["gla_w_gate"], w["gla_b_gate"], w["gla_out_norm"], w["dn_conv_w"], hist8, w["dn_a_log"],
        w["dn_dt_bias"], w["dn_out_norm"], jnp.swapaxes(s_gla, -1, -2), s_dn, C, min(AB_GROUP, L // C))
    h1 = _linear([(mix.reshape(n, _GV + _DW), 0)], w["ab_w_out"], res=(x2, 0), tm=tm, name="ab_out")

    W = SB_HEADS * SB_DH
    qkv, z, k_heads, v_heads = _sb_in(h1, w["sb_norm"], w["sb_w_in"], B, L, 256)
    qkv3 = qkv.reshape(B, L, 3 * W)
    nb = W // LANES
    if cache_k is None:
        o = _sb_attention(qkv3, 0, qkv3, nb, qkv3, 2 * nb, tq=tq, tk=tq, q_start=0, name="sb_prompt")
    else:
        past = cache_k.shape[2]
        tk = 256
        tk_pad = -(-(past + L) // tk) * tk
        pad = ((0, 0), (0, tk_pad - past - L), (0, 0))
        k_all = jnp.pad(jnp.concatenate([_from_heads(cache_k).astype(BF16), qkv3[:, :, W:2 * W]], axis=1), pad)
        v_all = jnp.pad(jnp.concatenate([_from_heads(cache_v).astype(BF16), qkv3[:, :, 2 * W:]], axis=1), pad)
        o = _sb_attention(qkv3, 0, k_all, 0, v_all, 0, tq=tq, tk=tk, q_start=past, name="sb_sample")
    y = _linear([(o.reshape(n, W), 0)], w["sb_w_out"], gate=(z, 0), res=(h1, 0),
                post=w["final_norm"], tm=tm, name="sb_out")
    return (y.reshape(B, L, D), jnp.swapaxes(s_gla_t, -1, -2), s_dn_new,
            conv8[:, SUBLANES - (CONV_W - 1):], k_heads, v_heads)


def kernel(x_prompt, x_sample, state_gla, state_delta, state_conv, cache_k, cache_v, ab_norm, ab_w_in,
           gla_w_gate, gla_b_gate, gla_out_norm, dn_conv_w, dn_a_log, dn_dt_bias, dn_out_norm, ab_w_out,
           sb_norm, sb_w_in, sb_w_out, final_norm):
    assert ab_w_in.shape[0] == 1 and sb_w_in.shape[0] == 1, "one (A|B) layer followed by one C layer"
    D = x_prompt.shape[-1]
    wi = ab_w_in[0]
    offs, acc = [], 0
    for s in (_GQ, _GQ, _GV, GLA_RANK, _GV, _DC, DN_HEADS, DN_HEADS, _DW):
        offs.append((acc, acc + s))
        acc += s
    gq, gk, gv, gr, gz, dqkv, dbeta, da, dz = [wi[:, a:b] for a, b in offs]
    small = jnp.concatenate([gr, dbeta, da], axis=1)
    w_ab = jnp.concatenate(
        [gq, gk, gv, gz, dqkv, dz, small,
         jnp.zeros((D, AB_COLS - COL_SM - small.shape[1]), wi.dtype)], axis=1).astype(BF16)
    wg = jnp.zeros((LANES, _GQ), F32).at[SM_RANK:SM_RANK + GLA_RANK].set(gla_w_gate[0]).astype(BF16)
    w = {
        "ab_norm": ab_norm[0], "ab_w_in": w_ab, "gla_w_gate": wg,
        "gla_b_gate": gla_b_gate[0].reshape(1, _GQ), "gla_out_norm": gla_out_norm[0].reshape(1, GLA_DV),
        "dn_conv_w": dn_conv_w[0], "dn_a_log": _pad_lanes(dn_a_log[0], SM_DEC),
        "dn_dt_bias": _pad_lanes(dn_dt_bias[0], SM_DEC), "dn_out_norm": dn_out_norm[0].reshape(1, DN_DV),
        "ab_w_out": ab_w_out[0].astype(BF16), "sb_norm": sb_norm[0], "sb_w_in": sb_w_in[0].astype(BF16),
        "sb_w_out": sb_w_out[0].astype(BF16), "final_norm": final_norm,
    }
    bp = x_prompt.shape[0]
    yp, gp, dp, cp, kp, vp = _group(
        x_prompt, jnp.zeros((bp, GLA_HEADS, GLA_DK, GLA_DV), F32), jnp.zeros((bp, DN_HEADS, DN_DK, DN_DV), F32),
        jnp.zeros((bp, CONV_W - 1, _DC), F32), None, None, CHUNK, 256, w)
    ys, gs, ds, cs, ks, vs = _group(
        x_sample, state_gla[0], state_delta[0], state_conv[0], cache_k[0], cache_v[0],
        x_sample.shape[1], x_sample.shape[1], w)
    st = lambda a: a[None]
    return (yp, ys, st(gp), st(gs), st(dp), st(ds), st(cp), st(cs), st(kp), st(ks), st(vp), st(vs))
```
